```python
import math
import jax, jax.numpy as jnp
from jax import lax
import numpy as np

D_MODEL = 1024
BATCH = 2
SEQ = 8192
DEPTH = 4
DEC_BATCH = 8
DEC_SEQ = 4096
PAST_LEN = 128

MIX_WIDTH = D_MODEL
DIFF_WIDTH = MIX_WIDTH // 2
RET_WIDTH = MIX_WIDTH - DIFF_WIDTH
DIFF_QK_DIM = 64
DIFF_V_DIM = 2 * DIFF_QK_DIM
DIFF_HEADS = DIFF_WIDTH // DIFF_V_DIM
RET_HEAD_DIM = 64
RET_HEADS = RET_WIDTH // RET_HEAD_DIM
ROT_DIMS = DIFF_QK_DIM // 4
ROPE_THETA = 500000.0
RET_THETA = 10000.0
D_FF = 4 * D_MODEL
PLE_DIM = 256
CHUNK = 128
Q_BLOCK = 128
EPS = 1e-6
IN_WIDTH = 3 * DIFF_WIDTH + 4 * RET_WIDTH
SPLITS = (DIFF_WIDTH, 2 * DIFF_WIDTH, 3 * DIFF_WIDTH,
          3 * DIFF_WIDTH + RET_WIDTH, 3 * DIFF_WIDTH + 2 * RET_WIDTH,
          3 * DIFF_WIDTH + 3 * RET_WIDTH)

kernel_name = 'hybrid_diffattn_retention_encoder'


def rms_norm(x, w):
    xf = x.astype(jnp.float32)
    y = xf * lax.rsqrt(jnp.mean(xf * xf, axis=-1, keepdims=True) + EPS)
    return (y * w.astype(jnp.float32)).astype(x.dtype)


def rotary(x, rot_dims, theta):
    s = x.shape[-2]
    half = rot_dims // 2
    inv = 1.0 / (theta ** (jnp.arange(0, rot_dims, 2, dtype=jnp.float32) / rot_dims))
    ang = jnp.arange(s, dtype=jnp.float32)[:, None] * inv[None, :]
    cos, sin = jnp.cos(ang), jnp.sin(ang)
    xf = x.astype(jnp.float32)
    x1, x2, xp = xf[..., :half], xf[..., half:rot_dims], xf[..., rot_dims:]
    out = jnp.concatenate([x1 * cos - x2 * sin, x1 * sin + x2 * cos, xp], axis=-1)
    return out.astype(x.dtype)


def differential_attention(q, k, v, lam):
    b, h, _, s, d = q.shape
    q = q * (d ** -0.5)
    nq = s // Q_BLOCK
    qb = jnp.moveaxis(q.reshape(b, h, 2, nq, Q_BLOCK, d), 3, 0)

    def block(qi):
        sc = jnp.einsum('bhcqd,bhckd->bhcqk', qi, k).astype(jnp.float32)
        pr = jax.nn.softmax(sc, axis=-1)
        w = pr[:, :, 0] - lam * pr[:, :, 1]
        return jnp.einsum('bhqk,bhke->bhqe', w.astype(v.dtype), v)

    out = lax.map(block, qb)
    return jnp.moveaxis(out, 0, 2).reshape(b, h, s, v.shape[-1])


def retention_chunkwise(q, k, v, log_gamma, strict):
    q = q.astype(jnp.float32)
    k = k.astype(jnp.float32)
    v = v.astype(jnp.float32)
    b, h, s, dk = q.shape
    dv = v.shape[-1]
    n = s // CHUNK
    qc = q.reshape(b, h, n, CHUNK, dk)
    kc = k.reshape(b, h, n, CHUNK, dk)
    vc = v.reshape(b, h, n, CHUNK, dv)
    idx = jnp.arange(CHUNK, dtype=jnp.float32)
    rel = idx[:, None] - idx[None, :]
    mask = (rel > 0) if strict else (rel >= 0)
    lg = log_gamma[:, None, None]
    dmat = jnp.exp(jnp.where(mask[None], rel[None] * lg, -jnp.inf))
    scores = jnp.einsum('bhncd,bhnmd->bhncm', qc, kc) * dmat[None, :, None]
    inner = jnp.einsum('bhncm,bhnme->bhnce', scores, vc)
    lg1 = log_gamma[:, None]
    k_decay = jnp.exp((CHUNK - 1.0 - idx)[None, :] * lg1)
    kv = jnp.einsum('bhncd,bhnce->bhnde', kc * k_decay[None, :, None, :, None], vc)
    chunk_decay = jnp.exp(CHUNK * log_gamma)[None, :, None, None]

    def step(state, kv_i):
        return state * chunk_decay + kv_i, state

    init = jnp.zeros((b, h, dk, dv), jnp.float32)
    _, prev = lax.scan(step, init, jnp.moveaxis(kv, 2, 0))
    prev = jnp.moveaxis(prev, 0, 2)
    q_decay = jnp.exp((idx + 1.0)[None, :] * lg1)
    cross = jnp.einsum('bhncd,bhnde->bhnce', qc * q_decay[None, :, None, :, None], prev)
    return (inner + cross).reshape(b, h, s, dv)


def encoder_layer(x, p_i, layer_idx, ln1_w, w_in, q_norm_w, k_norm_w, lam_p, subln_w,
                  decay_logit, gn_w, w_out, ln2_w, w1, w2, wg, wp):
    b, s, _ = x.shape
    hn = rms_norm(x, ln1_w)
    proj = hn @ w_in
    dq, dk, dv, rq, rk, rv, rg = jnp.split(proj, SPLITS, axis=-1)

    dq = rms_norm(dq.reshape(b, s, DIFF_HEADS, 2, DIFF_QK_DIM), q_norm_w)
    dk = rms_norm(dk.reshape(b, s, DIFF_HEADS, 2, DIFF_QK_DIM), k_norm_w)
    dq = rotary(jnp.transpose(dq, (0, 2, 3, 1, 4)), ROT_DIMS, ROPE_THETA)
    dk = rotary(jnp.transpose(dk, (0, 2, 3, 1, 4)), ROT_DIMS, ROPE_THETA)
    dv = dv.reshape(b, s, DIFF_HEADS, DIFF_V_DIM).transpose(0, 2, 1, 3)
    lam_init = 0.8 - 0.6 * math.exp(-0.3 * layer_idx)
    lp = lam_p.astype(jnp.float32)
    lam = jnp.exp(jnp.sum(lp[0] * lp[1])) - jnp.exp(jnp.sum(lp[2] * lp[3])) + lam_init
    a = differential_attention(dq, dk, dv, lam)
    a = rms_norm(a, subln_w) * (1.0 - lam_init)
    a = a.transpose(0, 2, 1, 3).reshape(b, s, DIFF_WIDTH).astype(x.dtype)

    rq = rotary(rq.reshape(b, s, RET_HEADS, RET_HEAD_DIM).transpose(0, 2, 1, 3), RET_HEAD_DIM, RET_THETA)
    rk = rotary(rk.reshape(b, s, RET_HEADS, RET_HEAD_DIM).transpose(0, 2, 1, 3), RET_HEAD_DIM, RET_THETA)
    rk = rk * (RET_HEAD_DIM ** -0.5)
    rv = rv.reshape(b, s, RET_HEADS, RET_HEAD_DIM).transpose(0, 2, 1, 3)
    lg = jax.nn.log_sigmoid(decay_logit.astype(jnp.float32))
    fwd = retention_chunkwise(rq, rk, rv, lg[0], False)
    bwd = jnp.flip(retention_chunkwise(jnp.flip(rq, 2), jnp.flip(rk, 2), jnp.flip(rv, 2), lg[1], True), 2)
    r = rms_norm(fwd + bwd, gn_w).astype(x.dtype)
    r = r.transpose(0, 2, 1, 3).reshape(b, s, RET_WIDTH)
    r = jax.nn.silu(rg) * r

    x = x + jnp.concatenate([a, r], axis=-1) @ w_out

    h2 = rms_norm(x, ln2_w)
    x = x + jnp.square(jax.nn.relu(h2 @ w1)) @ w2

    x = x + jax.nn.sigmoid(x @ wg) * (p_i @ wp)
    return x


def run_trunk(x, p, ln1_w, w_in, diff_q_norm, diff_k_norm, diff_lambda, diff_subln,
              ret_decay_logit, ret_gn, w_out, ln2_w, w_mlp1, w_mlp2, w_ple_gate, w_ple_proj):
    for i in range(DEPTH):
        x = encoder_layer(x, p[i], i, ln1_w[i], w_in[i], diff_q_norm[i], diff_k_norm[i],
                          diff_lambda[i], diff_subln[i], ret_decay_logit[i], ret_gn[i],
                          w_out[i], ln2_w[i], w_mlp1[i], w_mlp2[i], w_ple_gate[i], w_ple_proj[i])
    return x


def setup_inputs(seed: int = 0) -> dict:
    key = jax.random.key(seed)
    ks = jax.random.split(key, 20)
    f32 = jnp.float32
    nrm = lambda k, shape, scale: jax.random.normal(k, shape, f32) * scale
    base_logit = jnp.log(2.0 ** (5.0 + jnp.arange(RET_HEADS, dtype=f32)) - 1.0)
    return {
        'x_prompt': nrm(ks[0], (BATCH, SEQ, D_MODEL), 1.0),
        'x_sample': nrm(ks[1], (DEC_BATCH, DEC_SEQ, D_MODEL), 1.0),
        'p_prompt': nrm(ks[2], (DEPTH, BATCH, SEQ, PLE_DIM), 1.0),
        'p_sample': nrm(ks[3], (DEPTH, DEC_BATCH, DEC_SEQ, PLE_DIM), 1.0),
        'ln1_w': 1.0 + nrm(ks[4], (DEPTH, D_MODEL), 0.02),
        'w_in': nrm(ks[5], (DEPTH, D_MODEL, IN_WIDTH), D_MODEL ** -0.5),
        'diff_q_norm': 1.0 + nrm(ks[6], (DEPTH, DIFF_QK_DIM), 0.02),
        'diff_k_norm': 1.0 + nrm(ks[7], (DEPTH, DIFF_QK_DIM), 0.02),
        'diff_lambda': nrm(ks[8], (DEPTH, 4, DIFF_QK_DIM), 0.1),
        'diff_subln': 1.0 + nrm(ks[9], (DEPTH, DIFF_V_DIM), 0.02),
        'ret_decay_logit': base_logit[None, None, :] + nrm(ks[10], (DEPTH, 2, RET_HEADS), 0.1),
        'ret_gn': 1.0 + nrm(ks[11], (DEPTH, RET_HEAD_DIM), 0.02),
        'w_out': nrm(ks[12], (DEPTH, MIX_WIDTH, D_MODEL), MIX_WIDTH ** -0.5),
        'ln2_w': 1.0 + nrm(ks[13], (DEPTH, D_MODEL), 0.02),
        'w_mlp1': nrm(ks[14], (DEPTH, D_MODEL, D_FF), D_MODEL ** -0.5),
        'w_mlp2': nrm(ks[15], (DEPTH, D_FF, D_MODEL), D_FF ** -0.5),
        'w_ple_gate': nrm(ks[16], (DEPTH, D_MODEL, D_MODEL), D_MODEL ** -0.5),
        'w_ple_proj': nrm(ks[17], (DEPTH, PLE_DIM, D_MODEL), PLE_DIM ** -0.5),
    }


def reference(x_prompt, x_sample, p_prompt, p_sample, ln1_w, w_in, diff_q_norm, diff_k_norm,
              diff_lambda, diff_subln, ret_decay_logit, ret_gn, w_out, ln2_w, w_mlp1, w_mlp2,
              w_ple_gate, w_ple_proj):
    y_prompt = run_trunk(x_prompt, p_prompt, ln1_w, w_in, diff_q_norm, diff_k_norm, diff_lambda,
                         diff_subln, ret_decay_logit, ret_gn, w_out, ln2_w, w_mlp1, w_mlp2,
                         w_ple_gate, w_ple_proj)
    y_sample = run_trunk(x_sample, p_sample, ln1_w, w_in, diff_q_norm, diff_k_norm, diff_lambda,
                         diff_subln, ret_decay_logit, ret_gn, w_out, ln2_w, w_mlp1, w_mlp2,
                         w_ple_gate, w_ple_proj)
    return (y_prompt, y_sample)
```

```python
import functools
import math

import jax
import jax.numpy as jnp
from jax import lax
from jax.experimental import pallas as pl
from jax.experimental.pallas import tpu as pltpu

F32 = jnp.float32
BF16 = jnp.bfloat16

D_MODEL = 1024
DIFF_WIDTH = 512
RET_WIDTH = 512
QK_DIM = 64
DV_DIM = 128
DIFF_HEADS = 4
RET_HEAD_DIM = 64
RET_HEADS = 8
RET_PAIRS = RET_HEADS // 2
ROT_DIMS = 16
ROPE_THETA = 500000.0
RET_THETA = 10000.0
D_FF = 4096
PLE_DIM = 256
CHUNK = 128
EPS = 1e-6
LOG2E = 1.4426950408889634
NEG_BIG = -1e30

VMEM_LIMIT_BYTES = 56 * 1024 * 1024
LANES = 128
V_ROWS = DV_DIM + 16

TOKEN_TILE = 512
Q_TILE = 512
KV_TILE = 256
TABLE_TILE = 1024


def _cparams(n_grid):
    return pltpu.CompilerParams(dimension_semantics=("arbitrary",) * n_grid,
                                vmem_limit_bytes=VMEM_LIMIT_BYTES)


def _const_spec(shape):
    nd = len(shape)
    return pl.BlockSpec(shape, lambda *_: (0,) * nd)


def _rope_kernel(invd_ref, invrt_ref, invr_ref, sgn_ref,
                 cosd_ref, sind_ref, cosrt_ref, sinrt_ref, cosr_ref, sinr_ref):
    ts = cosd_ref.shape[1]
    base = pl.program_id(0) * ts
    pos_d = (base + lax.broadcasted_iota(jnp.int32, (8, ts), 1)).astype(F32)
    ang_d = pos_d * invd_ref[...]
    cosd_ref[...] = jnp.cos(ang_d)
    sind_ref[...] = jnp.sin(ang_d)
    pos_r = (base + lax.broadcasted_iota(jnp.int32, (32, ts), 1)).astype(F32)
    ang_r = pos_r * invrt_ref[...]
    cosrt_ref[...] = jnp.cos(ang_r)
    sinrt_ref[...] = jnp.sin(ang_r)
    pos_t = (base + lax.broadcasted_iota(jnp.int32, (ts, LANES), 0)).astype(F32)
    ang_t = pos_t * invr_ref[0:1, :]
    cosr_ref[...] = jnp.cos(ang_t)
    sinr_ref[...] = jnp.sin(ang_t) * sgn_ref[0:1, :]


def _rope_tables(s_max):
    ts = min(TABLE_TILE, s_max)
    inv_d = 1.0 / (ROPE_THETA ** (jnp.arange(0, ROT_DIMS, 2, dtype=F32) / ROT_DIMS))
    inv_r = 1.0 / (RET_THETA ** (jnp.arange(0, RET_HEAD_DIM, 2, dtype=F32) / RET_HEAD_DIM))
    invd_b = jnp.broadcast_to(inv_d[:, None], (8, ts))
    invrt_b = jnp.broadcast_to(inv_r[:, None], (32, ts))
    invr_lane = jnp.broadcast_to(jnp.tile(inv_r, LANES // 32)[None, :], (8, LANES))
    sgn = jnp.where((jnp.arange(LANES) % RET_HEAD_DIM) < RET_HEAD_DIM // 2, -1.0, 1.0).astype(F32)
    sgn_b = jnp.broadcast_to(sgn[None, :], (8, LANES))
    return pl.pallas_call(
        _rope_kernel,
        grid=(s_max // ts,),
        in_specs=[_const_spec((8, ts)), _const_spec((32, ts)), _const_spec((8, LANES)),
                  _const_spec((8, LANES))],
        out_specs=[pl.BlockSpec((8, ts), lambda i: (0, i)), pl.BlockSpec((8, ts), lambda i: (0, i)),
                   pl.BlockSpec((32, ts), lambda i: (0, i)), pl.BlockSpec((32, ts), lambda i: (0, i)),
                   pl.BlockSpec((ts, LANES), lambda i: (i, 0)), pl.BlockSpec((ts, LANES), lambda i: (i, 0))],
        out_shape=[jax.ShapeDtypeStruct((8, s_max), F32), jax.ShapeDtypeStruct((8, s_max), F32),
                   jax.ShapeDtypeStruct((32, s_max), F32), jax.ShapeDtypeStruct((32, s_max), F32),
                   jax.ShapeDtypeStruct((s_max, LANES), F32), jax.ShapeDtypeStruct((s_max, LANES), F32)],
        compiler_params=_cparams(1),
        name="rope_tables",
    )(invd_b, invrt_b, invr_lane, sgn_b)


def _inproj_kernel(x_ref, ln_ref, wf_ref, wt_ref, qw_ref, kw_ref,
                   cd_ref, sd_ref, crt_ref, srt_ref, cr_ref, sr_ref,
                   qt_ref, k_ref, v_ref, rq_ref, rkt_ref, rv_ref, rg_ref):
    tm = x_ref.shape[0]
    x = x_ref[...]
    ms = jnp.mean(x * x, axis=-1, keepdims=True)
    hn = (x * lax.rsqrt(ms + EPS) * ln_ref[...]).astype(BF16)
    feat = lax.dot_general(wf_ref[...], hn, (((1,), (1,)), ((), ())), preferred_element_type=F32)
    tok = jnp.dot(hn, wt_ref[...], preferred_element_type=F32)

    cd = cd_ref[...]
    sd = sd_ref[...]

    def norm_rot(z, w):
        z3 = z.reshape(2 * DIFF_HEADS, QK_DIM, tm)
        gms = jnp.mean(z3 * z3, axis=1, keepdims=True)
        y = z3 * lax.rsqrt(gms + EPS) * w[None]
        y1 = y[:, 0:8]
        y2 = y[:, 8:16]
        out = jnp.concatenate([y1 * cd - y2 * sd, y1 * sd + y2 * cd, y[:, 16:]], axis=1)
        return out.reshape(DIFF_WIDTH, tm)

    qt_ref[...] = norm_rot(feat[0:512], qw_ref[...]).astype(BF16)
    k_ref[...] = norm_rot(feat[512:1024], kw_ref[...]).T.astype(BF16)

    v3 = feat[1024:1536].reshape(DIFF_HEADS, DV_DIM, tm).astype(BF16)
    v_ref[:, 0:DV_DIM, :] = v3
    v_ref[:, DV_DIM:V_ROWS, :] = jnp.ones((DIFF_HEADS, V_ROWS - DV_DIM, tm), BF16)

    half = RET_HEAD_DIM // 2
    rk3 = feat[1536:2048].reshape(RET_HEADS, RET_HEAD_DIM, tm)
    crt = crt_ref[...]
    srt = srt_ref[...]
    r1 = rk3[:, 0:half]
    r2 = rk3[:, half:]
    rk_rot = jnp.concatenate([r1 * crt - r2 * srt, r1 * srt + r2 * crt], axis=1)
    rkt_ref[...] = (rk_rot * (RET_HEAD_DIM ** -0.5)).reshape(RET_WIDTH, tm).astype(BF16)

    rq = tok[:, 0:512]
    lane = lax.broadcasted_iota(jnp.int32, rq.shape, 1)
    partner = jnp.where((lane % RET_HEAD_DIM) < half,
                        pltpu.roll(rq, RET_WIDTH - half, 1), pltpu.roll(rq, half, 1))
    cr = jnp.concatenate([cr_ref[...]] * (RET_WIDTH // LANES), axis=1)
    sr = jnp.concatenate([sr_ref[...]] * (RET_WIDTH // LANES), axis=1)
    rq_ref[...] = (rq * cr + partner * sr).astype(BF16)
    rv_ref[...] = tok[:, 512:1024].astype(BF16)
    rg_ref[...] = tok[:, 1024:1536]


def _inproj(x, ln1, wf, wt, qw_b, kw_b, tabs, layer, batch, seq):
    t = batch * seq
    tm = min(TOKEN_TILE, seq)
    nst = seq // tm
    cosd, sind, cosrt, sinrt, cosr, sinr = tabs
    lay3 = lambda i: (layer, 0, 0)
    in_specs = [
        pl.BlockSpec((tm, D_MODEL), lambda i: (i, 0)),
        pl.BlockSpec((None, 1, D_MODEL), lay3),
        pl.BlockSpec((None, 2048, D_MODEL), lay3),
        pl.BlockSpec((None, D_MODEL, 1536), lay3),
        pl.BlockSpec((None, QK_DIM, tm), lay3),
        pl.BlockSpec((None, QK_DIM, tm), lay3),
        pl.BlockSpec((8, tm), lambda i: (0, i % nst)),
        pl.BlockSpec((8, tm), lambda i: (0, i % nst)),
        pl.BlockSpec((32, tm), lambda i: (0, i % nst)),
        pl.BlockSpec((32, tm), lambda i: (0, i % nst)),
        pl.BlockSpec((tm, LANES), lambda i: (i % nst, 0)),
        pl.BlockSpec((tm, LANES), lambda i: (i % nst, 0)),
    ]
    feat_spec = pl.BlockSpec((None, 512, tm), lambda i: (i // nst, 0, i % nst))
    tok_spec = pl.BlockSpec((tm, 512), lambda i: (i, 0))
    out_specs = [
        feat_spec,
        tok_spec,
        pl.BlockSpec((None, DIFF_HEADS, V_ROWS, tm), lambda i: (i // nst, 0, 0, i % nst)),
        tok_spec,
        feat_spec,
        tok_spec,
        tok_spec,
    ]
    out_shape = [
        jax.ShapeDtypeStruct((batch, 512, seq), BF16),
        jax.ShapeDtypeStruct((t, 512), BF16),
        jax.ShapeDtypeStruct((batch, DIFF_HEADS, V_ROWS, seq), BF16),
        jax.ShapeDtypeStruct((t, 512), BF16),
        jax.ShapeDtypeStruct((batch, 512, seq), BF16),
        jax.ShapeDtypeStruct((t, 512), BF16),
        jax.ShapeDtypeStruct((t, 512), F32),
    ]
    return pl.pallas_call(
        _inproj_kernel, grid=(t // tm,), in_specs=in_specs, out_specs=out_specs, out_shape=out_shape,
        compiler_params=_cparams(1), name="inproj",
    )(x, ln1, wf, wt, qw_b, kw_b, cosd, sind, cosrt, sinrt, cosr, sinr)


def _attn_kernel(lp_ref, qt_ref, k_ref, v_ref, sw_ref, o_ref, acc1, acc2, m1, m2, *, lam_init, tk):
    tq = qt_ref.shape[1]
    seq = k_ref.shape[0]
    qt = qt_ref[...]
    row = lax.broadcasted_iota(jnp.int32, qt.shape, 0)
    zero = jnp.zeros_like(qt)
    qa = jnp.where(row < QK_DIM, qt, zero)
    qb = jnp.where(row >= QK_DIM, qt, zero)
    acc1[...] = jnp.zeros_like(acc1)
    acc2[...] = jnp.zeros_like(acc2)
    m1[...] = jnp.full_like(m1, NEG_BIG)
    m2[...] = jnp.full_like(m2, NEG_BIG)

    def body(j, carry):
        off = pl.multiple_of(j * tk, tk)
        kb = k_ref[pl.ds(off, tk), :]
        vb = v_ref[:, pl.ds(off, tk)]
        for qx, acc, m in ((qa, acc1, m1), (qb, acc2, m2)):
            s = jnp.dot(kb, qx, preferred_element_type=F32)
            m_old = m[...]
            m_new = jnp.maximum(m_old, jnp.max(s, axis=0, keepdims=True))
            p = jnp.exp2(s - m_new).astype(BF16)
            alpha = jnp.exp2(m_old - m_new)
            acc[...] = acc[...] * alpha + jnp.dot(vb, p, preferred_element_type=F32)
            m[...] = m_new
        return carry

    lax.fori_loop(0, seq // tk, body, 0)

    lp = lp_ref[...]
    lam = (jnp.exp(jnp.sum(lp[0:1] * lp[1:2], axis=1, keepdims=True))
           - jnp.exp(jnp.sum(lp[2:3] * lp[3:4], axis=1, keepdims=True)) + lam_init)
    a1 = acc1[...]
    a2 = acc2[...]
    n1 = a1[0:DV_DIM] / a1[DV_DIM:DV_DIM + 1]
    n2 = a2[0:DV_DIM] / a2[DV_DIM:DV_DIM + 1]
    out = n1 - lam * n2
    oms = jnp.mean(out * out, axis=0, keepdims=True)
    y = out * lax.rsqrt(oms + EPS) * sw_ref[...]
    o_ref[...] = y.T.astype(BF16)


def _diff_attention(lp, qt, k, v, sw_b, layer, lam_init, batch, seq):
    tq = min(Q_TILE, seq)
    tk = min(KV_TILE, seq)
    k3 = k.reshape(batch, seq, DIFF_WIDTH)
    kern = functools.partial(_attn_kernel, lam_init=lam_init, tk=tk)
    return pl.pallas_call(
        kern,
        grid=(batch, DIFF_HEADS, seq // tq),
        in_specs=[
            pl.BlockSpec((None, 4, QK_DIM), lambda b, h, i: (layer, 0, 0)),
            pl.BlockSpec((None, DV_DIM, tq), lambda b, h, i: (b, h, i)),
            pl.BlockSpec((None, seq, DV_DIM), lambda b, h, i: (b, 0, h)),
            pl.BlockSpec((None, None, V_ROWS, seq), lambda b, h, i: (b, h, 0, 0)),
            pl.BlockSpec((None, DV_DIM, tq), lambda b, h, i: (layer, 0, 0)),
        ],
        out_specs=pl.BlockSpec((None, tq, DV_DIM), lambda b, h, i: (b, i, h)),
        out_shape=jax.ShapeDtypeStruct((batch, seq, DIFF_WIDTH), BF16),
        scratch_shapes=[pltpu.VMEM((V_ROWS, tq), F32), pltpu.VMEM((V_ROWS, tq), F32),
                        pltpu.VMEM((1, tq), F32), pltpu.VMEM((1, tq), F32)],
        compiler_params=_cparams(3), name="diff_attention",
    )(lp, qt, k3, v, sw_b)


def _ret_kernel(lgl_ref, lgr_ref, lgs_ref, gnw_ref, q_ref, kt_ref, v_ref, g_ref, o_ref,
                qdf, qdb, kdf, kdb, cdf, cdb, dcomb, s_st, t_st, t_all):
    b = pl.program_id(0)
    ph = pl.program_id(1)
    j = pl.program_id(2)
    n_chunks = pl.num_programs(2)
    c = CHUNK

    @pl.when((b == 0) & (ph == 0) & (j == 0))
    def _tables():
        lgl = jax.nn.log_sigmoid(lgl_ref[...])
        lgr = jax.nn.log_sigmoid(lgr_ref[...])
        lgs = jax.nn.log_sigmoid(lgs_ref[...])
        pos_row = lax.broadcasted_iota(jnp.int32, (c, RET_WIDTH), 0).astype(F32)
        qdf[...] = jnp.exp((pos_row + 1.0) * lgl[0, 0:1, :])
        qdb[...] = jnp.exp((c - pos_row) * lgl[1, 0:1, :])
        pos_lane = lax.broadcasted_iota(jnp.int32, (RET_WIDTH, c), 1).astype(F32)
        kdf[...] = jnp.exp((c - 1.0 - pos_lane) * lgr[0])
        kdb[...] = jnp.exp(pos_lane * lgr[1])
        cdf[...] = jnp.exp(float(c) * lgr[0])
        cdb[...] = jnp.exp(float(c) * lgr[1])
        qi = (lax.broadcasted_iota(jnp.int32, (RET_HEADS * c, c), 0) % c).astype(F32)
        ki = lax.broadcasted_iota(jnp.int32, (RET_HEADS * c, c), 1).astype(F32)
        rel = qi - ki
        dcomb[...] = jnp.where(rel >= 0, jnp.exp(rel * lgs[0]), jnp.exp(-rel * lgs[1]))

    lane = lax.broadcasted_iota(jnp.int32, (c, c), 1)
    row = lax.broadcasted_iota(jnp.int32, (c, c), 0)
    lo = lane < RET_HEAD_DIM
    same_head = (row < RET_HEAD_DIM) == lo

    @pl.when((ph == 0) & (j == 0))
    def _init():
        t_st[...] = jnp.zeros_like(t_st)
        s_st[...] = jnp.zeros_like(s_st)

    @pl.when(ph == 0)
    def _backward_states():
        jj = n_chunks - 1 - j
        for p in range(RET_PAIRS):
            sl = slice(p * c, (p + 1) * c)
            t_old = t_st[p]
            t_all[jj, p] = t_old.astype(BF16)
            ktd = (kt_ref[sl, :].astype(F32) * kdb[sl, :]).astype(BF16)
            kv = jnp.dot(ktd, v_ref[:, sl], preferred_element_type=F32)
            t_st[p] = t_old * cdb[sl, :] + jnp.where(same_head, kv, 0.0)

    @pl.when(ph == 1)
    def _forward():
        gmat = jnp.where(same_head, 1.0 / RET_HEAD_DIM, 0.0).astype(BF16)
        for p in range(RET_PAIRS):
            sl = slice(p * c, (p + 1) * c)
            qp = q_ref[:, sl]
            ktp = kt_ref[sl, :]
            vp = v_ref[:, sl]
            zq = jnp.zeros_like(qp)
            q2 = jnp.concatenate([jnp.where(lo, qp, zq), jnp.where(lo, zq, qp)], axis=0)
            s = jnp.dot(q2, ktp, preferred_element_type=F32)
            sd = (s * dcomb[2 * p * c:2 * (p + 1) * c, :]).astype(BF16)
            v2 = jnp.concatenate([jnp.where(lo, vp, zq), jnp.where(lo, zq, vp)], axis=0)
            inner = jnp.dot(jnp.concatenate([sd[0:c], sd[c:2 * c]], axis=1), v2,
                            preferred_element_type=F32)
            qf = qp.astype(F32)
            qdec = jnp.concatenate([(qf * qdf[:, sl]).astype(BF16), (qf * qdb[:, sl]).astype(BF16)], axis=1)
            s_old = s_st[p]
            st = jnp.concatenate([s_old.astype(BF16), t_all[j, p]], axis=0)
            o = inner + jnp.dot(qdec, st, preferred_element_type=F32)
            ktd = (ktp.astype(F32) * kdf[sl, :]).astype(BF16)
            kv = jnp.dot(ktd, vp, preferred_element_type=F32)
            s_st[p] = s_old * cdf[sl, :] + jnp.where(same_head, kv, 0.0)
            o2 = o * o
            o2h = o2.astype(BF16)
            o2l = (o2 - o2h.astype(F32)).astype(BF16)
            gms = (jnp.dot(o2h, gmat, preferred_element_type=F32)
                   + jnp.dot(o2l, gmat, preferred_element_type=F32))
            r = o * lax.rsqrt(gms + EPS) * gnw_ref[0:1, :]
            g = g_ref[:, sl]
            o_ref[:, sl] = (g * jax.nn.sigmoid(g) * r).astype(BF16)


def _retention(lg_lane, lg_row, lg_sc, gnw_b, rq, rkt, rv, rg, layer, batch, seq):
    n = seq // CHUNK
    c = CHUNK

    def tok_map(b, ph, j):
        return (b * n + jnp.where(ph == 0, n - 1 - j, j), 0)

    def feat_map(b, ph, j):
        return (b, 0, jnp.where(ph == 0, n - 1 - j, j))

    def fwd_only_map(b, ph, j):
        return (b * n + jnp.where(ph == 0, 0, j), 0)

    return pl.pallas_call(
        _ret_kernel,
        grid=(batch, 2, n),
        in_specs=[
            pl.BlockSpec((None, 2, 8, RET_WIDTH), lambda b, ph, j: (layer, 0, 0, 0)),
            pl.BlockSpec((None, 2, RET_WIDTH, c), lambda b, ph, j: (layer, 0, 0, 0)),
            pl.BlockSpec((None, 2, RET_HEADS * c, c), lambda b, ph, j: (layer, 0, 0, 0)),
            pl.BlockSpec((None, 8, LANES), lambda b, ph, j: (layer, 0, 0)),
            pl.BlockSpec((c, RET_WIDTH), fwd_only_map),
            pl.BlockSpec((None, RET_WIDTH, c), feat_map),
            pl.BlockSpec((c, RET_WIDTH), tok_map),
            pl.BlockSpec((c, RET_WIDTH), fwd_only_map),
        ],
        out_specs=pl.BlockSpec((c, RET_WIDTH), fwd_only_map),
        out_shape=jax.ShapeDtypeStruct((batch * seq, RET_WIDTH), BF16),
        scratch_shapes=[
            pltpu.VMEM((c, RET_WIDTH), F32), pltpu.VMEM((c, RET_WIDTH), F32),
            pltpu.VMEM((RET_WIDTH, c), F32), pltpu.VMEM((RET_WIDTH, c), F32),
            pltpu.VMEM((RET_WIDTH, c), F32), pltpu.VMEM((RET_WIDTH, c), F32),
            pltpu.VMEM((RET_HEADS * c, c), F32),
            pltpu.VMEM((RET_PAIRS, c, c), F32), pltpu.VMEM((RET_PAIRS, c, c), F32),
            pltpu.VMEM((n, RET_PAIRS, c, c), BF16),
        ],
        compiler_params=_cparams(3), name="retention",
    )(lg_lane, lg_row, lg_sc, gnw_b, rq, rkt, rv, rg)


def _post_kernel(x_ref, a_ref, r_ref, p_ref, wo_ref, ln2_ref, w1_ref, w2_ref, wg_ref, wp_ref, o_ref):
    x = x_ref[...]
    x = x + jnp.dot(a_ref[...], wo_ref[0:DIFF_WIDTH, :], preferred_element_type=F32)
    x = x + jnp.dot(r_ref[...], wo_ref[DIFF_WIDTH:, :], preferred_element_type=F32)
    ms = jnp.mean(x * x, axis=-1, keepdims=True)
    h2 = (x * lax.rsqrt(ms + EPS) * ln2_ref[...]).astype(BF16)
    hid = jnp.dot(h2, w1_ref[...], preferred_element_type=F32)
    act = jnp.square(jnp.maximum(hid, 0.0)).astype(BF16)
    x = x + jnp.dot(act, w2_ref[...], preferred_element_type=F32)
    gate = jax.nn.sigmoid(jnp.dot(x.astype(BF16), wg_ref[...], preferred_element_type=F32))
    pe = jnp.dot(p_ref[...].astype(BF16), wp_ref[...], preferred_element_type=F32)
    o_ref[...] = x + gate * pe


def _post(x, a, r, p, wo, ln2, w1, w2, wg, wp, layer, batch, seq):
    t = batch * seq
    tm = min(TOKEN_TILE, seq)
    lay3 = lambda i: (layer, 0, 0)

    def wspec(shape):
        return pl.BlockSpec((None,) + shape, lay3, pipeline_mode=pl.Buffered(1))

    return pl.pallas_call(
        _post_kernel,
        grid=(t // tm,),
        in_specs=[
            pl.BlockSpec((tm, D_MODEL), lambda i: (i, 0)),
            pl.BlockSpec((tm, DIFF_WIDTH), lambda i: (i, 0)),
            pl.BlockSpec((tm, RET_WIDTH), lambda i: (i, 0)),
            pl.BlockSpec((None, tm, PLE_DIM), lambda i: (layer, i, 0)),
            wspec((D_MODEL, D_MODEL)),
            pl.BlockSpec((None, 1, D_MODEL), lay3),
            wspec((D_MODEL, D_FF)),
            wspec((D_FF, D_MODEL)),
            wspec((D_MODEL, D_MODEL)),
            wspec((PLE_DIM, D_MODEL)),
        ],
        out_specs=pl.BlockSpec((tm, D_MODEL), lambda i: (i, 0)),
        out_shape=jax.ShapeDtypeStruct((t, D_MODEL), F32),
        compiler_params=_cparams(1), name="post",
    )(x, a, r, p, wo, ln2, w1, w2, wg, wp)


def _prepare_params(ln1_w, w_in, diff_q_norm, diff_k_norm, diff_subln, ret_decay_logit, ret_gn,
                    w_out, ln2_w, w_mlp1, w_mlp2, w_ple_gate, w_ple_proj):
    depth = w_in.shape[0]
    tm = TOKEN_TILE
    dq, dk, dv = w_in[:, :, 0:512], w_in[:, :, 512:1024], w_in[:, :, 1024:1536]
    rq, rk = w_in[:, :, 1536:2048], w_in[:, :, 2048:2560]
    rv, rg = w_in[:, :, 2560:3072], w_in[:, :, 3072:3584]
    wf = jnp.swapaxes(jnp.concatenate([dq, dk, dv, rk], axis=2), 1, 2).astype(BF16)
    wt = jnp.concatenate([rq, rv, rg], axis=2).astype(BF16)
    q_scale = (QK_DIM ** -0.5) * LOG2E
    qw_b = jnp.broadcast_to((diff_q_norm.astype(F32) * q_scale)[:, :, None], (depth, QK_DIM, tm))
    kw_b = jnp.broadcast_to(diff_k_norm.astype(F32)[:, :, None], (depth, QK_DIM, tm))
    lam_init = [0.8 - 0.6 * math.exp(-0.3 * i) for i in range(depth)]
    sub_scale = jnp.asarray([1.0 - li for li in lam_init], F32)[:, None]
    sw_b = jnp.broadcast_to((diff_subln.astype(F32) * sub_scale)[:, :, None], (depth, DV_DIM, Q_TILE))
    lgt = ret_decay_logit.astype(F32)
    lg_lane = jnp.broadcast_to(jnp.repeat(lgt, RET_HEAD_DIM, axis=2)[:, :, None, :], (depth, 2, 8, RET_WIDTH))
    lg_row = jnp.broadcast_to(jnp.repeat(lgt, RET_HEAD_DIM, axis=2)[:, :, :, None], (depth, 2, RET_WIDTH, CHUNK))
    lg_sc = jnp.broadcast_to(jnp.repeat(lgt, CHUNK, axis=2)[:, :, :, None], (depth, 2, RET_HEADS * CHUNK, CHUNK))
    gnw_b = jnp.broadcast_to(jnp.tile(ret_gn.astype(F32), (1, 2))[:, None, :], (depth, 8, LANES))
    return dict(
        ln1=ln1_w.astype(F32)[:, None, :], wf=wf, wt=wt, qw_b=qw_b, kw_b=kw_b, sw_b=sw_b, lam_init=lam_init,
        lg_lane=lg_lane, lg_row=lg_row, lg_sc=lg_sc, gnw_b=gnw_b,
        wo=w_out.astype(BF16), ln2=ln2_w.astype(F32)[:, None, :], w1=w_mlp1.astype(BF16),
        w2=w_mlp2.astype(BF16), wg=w_ple_gate.astype(BF16), wp=w_ple_proj.astype(BF16))


def _run_trunk(x, p, prm, tabs, diff_lambda):
    batch, seq, _ = x.shape
    depth = p.shape[0]
    xf = x.reshape(batch * seq, D_MODEL)
    pf = p.reshape(depth, batch * seq, PLE_DIM)
    lp = diff_lambda.astype(F32)
    for i in range(depth):
        qt, k, v, rq, rkt, rv, rg = _inproj(xf, prm["ln1"], prm["wf"], prm["wt"], prm["qw_b"], prm["kw_b"],
                                            tabs, i, batch, seq)
        a = _diff_attention(lp, qt, k, v, prm["sw_b"], i, prm["lam_init"][i], batch, seq)
        r = _retention(prm["lg_lane"], prm["lg_row"], prm["lg_sc"], prm["gnw_b"], rq, rkt, rv, rg,
                       i, batch, seq)
        xf = _post(xf, a.reshape(batch * seq, DIFF_WIDTH), r, pf, prm["wo"], prm["ln2"], prm["w1"],
                   prm["w2"], prm["wg"], prm["wp"], i, batch, seq)
    return xf.reshape(batch, seq, D_MODEL)


def kernel(x_prompt, x_sample, p_prompt, p_sample, ln1_w, w_in, diff_q_norm, diff_k_norm, diff_lambda,
           diff_subln, ret_decay_logit, ret_gn, w_out, ln2_w, w_mlp1, w_mlp2, w_ple_gate, w_ple_proj):
    prm = _prepare_params(ln1_w, w_in, diff_q_norm, diff_k_norm, diff_subln, ret_decay_logit, ret_gn,
                          w_out, ln2_w, w_mlp1, w_mlp2, w_ple_gate, w_ple_proj)
    tabs = _rope_tables(max(x_prompt.shape[1], x_sample.shape[1]))
    y_prompt = _run_trunk(x_prompt, p_prompt, prm, tabs, diff_lambda)
    y_sample = _run_trunk(x_sample, p_sample, prm, tabs, diff_lambda)
    return (y_prompt, y_sample)
```

```python
import functools
import math

import jax
import jax.numpy as jnp
from jax import lax
from jax.experimental import pallas as pl
from jax.experimental.pallas import tpu as pltpu

F32 = jnp.float32
BF16 = jnp.bfloat16

D_MODEL = 1024
DIFF_WIDTH = 512
RET_WIDTH = 512
QK_DIM = 64
DV_DIM = 128
DIFF_HEADS = 4
RET_HEAD_DIM = 64
RET_HEADS = 8
RET_PAIRS = RET_HEADS // 2
ROT_DIMS = 16
ROPE_THETA = 500000.0
RET_THETA = 10000.0
D_FF = 4096
PLE_DIM = 256
CHUNK = 128
EPS = 1e-6
LOG2E = 1.4426950408889634
NEG_BIG = -1e30

VMEM_LIMIT_BYTES = 56 * 1024 * 1024
LANES = 128
V_ROWS = DV_DIM + 16

TOKEN_TILE = 512
Q_TILE = 1024
KV_TILE = 256
KV_TILE_FAST = 2048
BOUND_SLACK = 1.02
UNDERFLOW_GUARD = 2.0 ** -64
TABLE_TILE = 1024


def _cparams(n_grid):
    return pltpu.CompilerParams(dimension_semantics=("arbitrary",) * n_grid,
                                vmem_limit_bytes=VMEM_LIMIT_BYTES)


def _const_spec(shape):
    nd = len(shape)
    return pl.BlockSpec(shape, lambda *_: (0,) * nd)


def _rope_kernel(invd_ref, invrt_ref, invr_ref, sgn_ref,
                 cosd_ref, sind_ref, cosrt_ref, sinrt_ref, cosr_ref, sinr_ref):
    ts = cosd_ref.shape[1]
    base = pl.program_id(0) * ts
    pos_d = (base + lax.broadcasted_iota(jnp.int32, (8, ts), 1)).astype(F32)
    ang_d = pos_d * invd_ref[...]
    cosd_ref[...] = jnp.cos(ang_d)
    sind_ref[...] = jnp.sin(ang_d)
    pos_r = (base + lax.broadcasted_iota(jnp.int32, (32, ts), 1)).astype(F32)
    ang_r = pos_r * invrt_ref[...]
    cosrt_ref[...] = jnp.cos(ang_r)
    sinrt_ref[...] = jnp.sin(ang_r)
    pos_t = (base + lax.broadcasted_iota(jnp.int32, (ts, LANES), 0)).astype(F32)
    ang_t = pos_t * invr_ref[0:1, :]
    cosr_ref[...] = jnp.cos(ang_t)
    sinr_ref[...] = jnp.sin(ang_t) * sgn_ref[0:1, :]


def _rope_tables(s_max):
    ts = min(TABLE_TILE, s_max)
    inv_d = 1.0 / (ROPE_THETA ** (jnp.arange(0, ROT_DIMS, 2, dtype=F32) / ROT_DIMS))
    inv_r = 1.0 / (RET_THETA ** (jnp.arange(0, RET_HEAD_DIM, 2, dtype=F32) / RET_HEAD_DIM))
    invd_b = jnp.broadcast_to(inv_d[:, None], (8, ts))
    invrt_b = jnp.broadcast_to(inv_r[:, None], (32, ts))
    invr_lane = jnp.broadcast_to(jnp.tile(inv_r, LANES // 32)[None, :], (8, LANES))
    sgn = jnp.where((jnp.arange(LANES) % RET_HEAD_DIM) < RET_HEAD_DIM // 2, -1.0, 1.0).astype(F32)
    sgn_b = jnp.broadcast_to(sgn[None, :], (8, LANES))
    return pl.pallas_call(
        _rope_kernel,
        grid=(s_max // ts,),
        in_specs=[_const_spec((8, ts)), _const_spec((32, ts)), _const_spec((8, LANES)),
                  _const_spec((8, LANES))],
        out_specs=[pl.BlockSpec((8, ts), lambda i: (0, i)), pl.BlockSpec((8, ts), lambda i: (0, i)),
                   pl.BlockSpec((32, ts), lambda i: (0, i)), pl.BlockSpec((32, ts), lambda i: (0, i)),
                   pl.BlockSpec((ts, LANES), lambda i: (i, 0)), pl.BlockSpec((ts, LANES), lambda i: (i, 0))],
        out_shape=[jax.ShapeDtypeStruct((8, s_max), F32), jax.ShapeDtypeStruct((8, s_max), F32),
                   jax.ShapeDtypeStruct((32, s_max), F32), jax.ShapeDtypeStruct((32, s_max), F32),
                   jax.ShapeDtypeStruct((s_max, LANES), F32), jax.ShapeDtypeStruct((s_max, LANES), F32)],
        compiler_params=_cparams(1),
        name="rope_tables",
    )(invd_b, invrt_b, invr_lane, sgn_b)


def _inproj_kernel(x_ref, ln_ref, wf_ref, wt_ref, qw_ref, kw_ref,
                   cd_ref, sd_ref, crt_ref, srt_ref, cr_ref, sr_ref,
                   qt_ref, k_ref, v_ref, rq_ref, rkt_ref, rv_ref, rg_ref):
    tm = x_ref.shape[0]
    x = x_ref[...]
    ms = jnp.mean(x * x, axis=-1, keepdims=True)
    hn = (x * lax.rsqrt(ms + EPS) * ln_ref[...]).astype(BF16)
    feat = lax.dot_general(wf_ref[...], hn, (((1,), (1,)), ((), ())), preferred_element_type=F32)
    tok = jnp.dot(hn, wt_ref[...], preferred_element_type=F32)

    cd = cd_ref[...]
    sd = sd_ref[...]

    def norm_rot(z, w):
        z3 = z.reshape(2 * DIFF_HEADS, QK_DIM, tm)
        gms = jnp.mean(z3 * z3, axis=1, keepdims=True)
        y = z3 * lax.rsqrt(gms + EPS) * w[None]
        y1 = y[:, 0:8]
        y2 = y[:, 8:16]
        out = jnp.concatenate([y1 * cd - y2 * sd, y1 * sd + y2 * cd, y[:, 16:]], axis=1)
        return out.reshape(DIFF_WIDTH, tm)

    qt_ref[...] = norm_rot(feat[0:512], qw_ref[...]).astype(BF16)
    k_ref[...] = norm_rot(feat[512:1024], kw_ref[...]).T.astype(BF16)

    v3 = feat[1024:1536].reshape(DIFF_HEADS, DV_DIM, tm).astype(BF16)
    v_ref[:, 0:DV_DIM, :] = v3
    v_ref[:, DV_DIM:V_ROWS, :] = jnp.ones((DIFF_HEADS, V_ROWS - DV_DIM, tm), BF16)

    half = RET_HEAD_DIM // 2
    rk3 = feat[1536:2048].reshape(RET_HEADS, RET_HEAD_DIM, tm)
    crt = crt_ref[...]
    srt = srt_ref[...]
    r1 = rk3[:, 0:half]
    r2 = rk3[:, half:]
    rk_rot = jnp.concatenate([r1 * crt - r2 * srt, r1 * srt + r2 * crt], axis=1)
    rkt_ref[...] = (rk_rot * (RET_HEAD_DIM ** -0.5)).reshape(RET_WIDTH, tm).astype(BF16)

    rq = tok[:, 0:512]
    lane = lax.broadcasted_iota(jnp.int32, rq.shape, 1)
    partner = jnp.where((lane % RET_HEAD_DIM) < half,
                        pltpu.roll(rq, RET_WIDTH - half, 1), pltpu.roll(rq, half, 1))
    cr = jnp.concatenate([cr_ref[...]] * (RET_WIDTH // LANES), axis=1)
    sr = jnp.concatenate([sr_ref[...]] * (RET_WIDTH // LANES), axis=1)
    rq_ref[...] = (rq * cr + partner * sr).astype(BF16)
    rv_ref[...] = tok[:, 512:1024].astype(BF16)
    rg_ref[...] = tok[:, 1024:1536]


def _inproj(x, ln1, wf, wt, qw_b, kw_b, tabs, layer, batch, seq):
    t = batch * seq
    tm = min(TOKEN_TILE, seq)
    nst = seq // tm
    cosd, sind, cosrt, sinrt, cosr, sinr = tabs
    lay3 = lambda i: (layer, 0, 0)
    in_specs = [
        pl.BlockSpec((tm, D_MODEL), lambda i: (i, 0)),
        pl.BlockSpec((None, 1, D_MODEL), lay3),
        pl.BlockSpec((None, 2048, D_MODEL), lay3),
        pl.BlockSpec((None, D_MODEL, 1536), lay3),
        pl.BlockSpec((None, QK_DIM, tm), lay3),
        pl.BlockSpec((None, QK_DIM, tm), lay3),
        pl.BlockSpec((8, tm), lambda i: (0, i % nst)),
        pl.BlockSpec((8, tm), lambda i: (0, i % nst)),
        pl.BlockSpec((32, tm), lambda i: (0, i % nst)),
        pl.BlockSpec((32, tm), lambda i: (0, i % nst)),
        pl.BlockSpec((tm, LANES), lambda i: (i % nst, 0)),
        pl.BlockSpec((tm, LANES), lambda i: (i % nst, 0)),
    ]
    feat_spec = pl.BlockSpec((None, 512, tm), lambda i: (i // nst, 0, i % nst))
    tok_spec = pl.BlockSpec((tm, 512), lambda i: (i, 0))
    out_specs = [
        feat_spec,
        tok_spec,
        pl.BlockSpec((None, DIFF_HEADS, V_ROWS, tm), lambda i: (i // nst, 0, 0, i % nst)),
        tok_spec,
        feat_spec,
        tok_spec,
        tok_spec,
    ]
    out_shape = [
        jax.ShapeDtypeStruct((batch, 512, seq), BF16),
        jax.ShapeDtypeStruct((t, 512), BF16),
        jax.ShapeDtypeStruct((batch, DIFF_HEADS, V_ROWS, seq), BF16),
        jax.ShapeDtypeStruct((t, 512), BF16),
        jax.ShapeDtypeStruct((batch, 512, seq), BF16),
        jax.ShapeDtypeStruct((t, 512), BF16),
        jax.ShapeDtypeStruct((t, 512), F32),
    ]
    return pl.pallas_call(
        _inproj_kernel, grid=(t // tm,), in_specs=in_specs, out_specs=out_specs, out_shape=out_shape,
        compiler_params=_cparams(1), name="inproj",
    )(x, ln1, wf, wt, qw_b, kw_b, cosd, sind, cosrt, sinrt, cosr, sinr)


def _attn_kernel(lp_ref, qw_ref, kw_ref, qt_ref, k_ref, v_ref, sw_ref, o_ref, acc1, acc2, m1, m2,
                 *, lam_init, tk, tk_fast):
    seq = k_ref.shape[0]
    qt = qt_ref[...]
    row = lax.broadcasted_iota(jnp.int32, qt.shape, 0)
    zero = jnp.zeros_like(qt)
    qa = jnp.where(row < QK_DIM, qt, zero)
    qb = jnp.where(row >= QK_DIM, qt, zero)
    comps = ((qa, acc1, m1), (qb, acc2, m2))

    bound = (jnp.max(jnp.abs(qw_ref[...]), axis=1, keepdims=True)
             * jnp.max(jnp.abs(kw_ref[...]), axis=1, keepdims=True) * (QK_DIM * BOUND_SLACK))
    acc1[...] = jnp.zeros_like(acc1)
    acc2[...] = jnp.zeros_like(acc2)

    def bound_body(j, carry):
        off = pl.multiple_of(j * tk_fast, tk_fast)
        kb = k_ref[pl.ds(off, tk_fast), :]
        vb = v_ref[:, pl.ds(off, tk_fast)]
        for qx, acc, _ in comps:
            s = jnp.dot(kb, qx, preferred_element_type=F32)
            p = jnp.exp2(s - bound).astype(BF16)
            acc[...] += jnp.dot(vb, p, preferred_element_type=F32)
        return carry

    lax.fori_loop(0, seq // tk_fast, bound_body, 0)
    l_min = jnp.minimum(jnp.min(acc1[DV_DIM:DV_DIM + 1, :]), jnp.min(acc2[DV_DIM:DV_DIM + 1, :]))

    @pl.when(l_min < UNDERFLOW_GUARD)
    def _running_max_pass():
        acc1[...] = jnp.zeros_like(acc1)
        acc2[...] = jnp.zeros_like(acc2)
        m1[...] = jnp.full_like(m1, NEG_BIG)
        m2[...] = jnp.full_like(m2, NEG_BIG)

        def body(j, carry):
            off = pl.multiple_of(j * tk, tk)
            kb = k_ref[pl.ds(off, tk), :]
            vb = v_ref[:, pl.ds(off, tk)]
            for qx, acc, m in comps:
                s = jnp.dot(kb, qx, preferred_element_type=F32)
                m_old = m[...]
                m_new = jnp.maximum(m_old, jnp.max(s, axis=0, keepdims=True))
                p = jnp.exp2(s - m_new).astype(BF16)
                alpha = jnp.exp2(m_old - m_new)
                acc[...] = acc[...] * alpha + jnp.dot(vb, p, preferred_element_type=F32)
                m[...] = m_new
            return carry

        lax.fori_loop(0, seq // tk, body, 0)

    lp = lp_ref[...]
    lam = (jnp.exp(jnp.sum(lp[0:1] * lp[1:2], axis=1, keepdims=True))
           - jnp.exp(jnp.sum(lp[2:3] * lp[3:4], axis=1, keepdims=True)) + lam_init)
    a1 = acc1[...]
    a2 = acc2[...]
    n1 = a1[0:DV_DIM] / a1[DV_DIM:DV_DIM + 1]
    n2 = a2[0:DV_DIM] / a2[DV_DIM:DV_DIM + 1]
    out = n1 - lam * n2
    oms = jnp.mean(out * out, axis=0, keepdims=True)
    y = out * lax.rsqrt(oms + EPS) * sw_ref[...]
    o_ref[...] = y.T.astype(BF16)


def _diff_attention(lp, qw, kw, qt, k, v, sw_b, layer, lam_init, batch, seq):
    tq = min(Q_TILE, seq)
    tk = min(KV_TILE, seq)
    tk_fast = min(KV_TILE_FAST, seq)
    k3 = k.reshape(batch, seq, DIFF_WIDTH)
    kern = functools.partial(_attn_kernel, lam_init=lam_init, tk=tk, tk_fast=tk_fast)
    lay3 = lambda b, h, i: (layer, 0, 0)
    return pl.pallas_call(
        kern,
        grid=(batch, DIFF_HEADS, seq // tq),
        in_specs=[
            pl.BlockSpec((None, 4, QK_DIM), lay3),
            pl.BlockSpec((None, 1, QK_DIM), lay3),
            pl.BlockSpec((None, 1, QK_DIM), lay3),
            pl.BlockSpec((None, DV_DIM, tq), lambda b, h, i: (b, h, i)),
            pl.BlockSpec((None, seq, DV_DIM), lambda b, h, i: (b, 0, h)),
            pl.BlockSpec((None, None, V_ROWS, seq), lambda b, h, i: (b, h, 0, 0)),
            pl.BlockSpec((None, DV_DIM, tq), lay3),
        ],
        out_specs=pl.BlockSpec((None, tq, DV_DIM), lambda b, h, i: (b, i, h)),
        out_shape=jax.ShapeDtypeStruct((batch, seq, DIFF_WIDTH), BF16),
        scratch_shapes=[pltpu.VMEM((V_ROWS, tq), F32), pltpu.VMEM((V_ROWS, tq), F32),
                        pltpu.VMEM((1, tq), F32), pltpu.VMEM((1, tq), F32)],
        compiler_params=_cparams(3), name="diff_attention",
    )(lp, qw, kw, qt, k3, v, sw_b)


def _ret_kernel(lgl_ref, lgr_ref, lgs_ref, gnw_ref, q_ref, kt_ref, v_ref, g_ref, o_ref,
                qdf, qdb, kdf, kdb, cdf, cdb, dcomb, s_st, t_st, t_all):
    b = pl.program_id(0)
    ph = pl.program_id(1)
    j = pl.program_id(2)
    n_chunks = pl.num_programs(2)
    c = CHUNK

    @pl.when((b == 0) & (ph == 0) & (j == 0))
    def _tables():
        lgl = jax.nn.log_sigmoid(lgl_ref[...])
        lgr = jax.nn.log_sigmoid(lgr_ref[...])
        lgs = jax.nn.log_sigmoid(lgs_ref[...])
        pos_row = lax.broadcasted_iota(jnp.int32, (c, RET_WIDTH), 0).astype(F32)
        qdf[...] = jnp.exp((pos_row + 1.0) * lgl[0, 0:1, :])
        qdb[...] = jnp.exp((c - pos_row) * lgl[1, 0:1, :])
        pos_lane = lax.broadcasted_iota(jnp.int32, (RET_WIDTH, c), 1).astype(F32)
        kdf[...] = jnp.exp((c - 1.0 - pos_lane) * lgr[0])
        kdb[...] = jnp.exp(pos_lane * lgr[1])
        cdf[...] = jnp.exp(float(c) * lgr[0])
        cdb[...] = jnp.exp(float(c) * lgr[1])
        qi = (lax.broadcasted_iota(jnp.int32, (RET_HEADS * c, c), 0) % c).astype(F32)
        ki = lax.broadcasted_iota(jnp.int32, (RET_HEADS * c, c), 1).astype(F32)
        rel = qi - ki
        dcomb[...] = jnp.where(rel >= 0, jnp.exp(rel * lgs[0]), jnp.exp(-rel * lgs[1]))

    lane = lax.broadcasted_iota(jnp.int32, (c, c), 1)
    row = lax.broadcasted_iota(jnp.int32, (c, c), 0)
    lo = lane < RET_HEAD_DIM
    same_head = (row < RET_HEAD_DIM) == lo

    @pl.when((ph == 0) & (j == 0))
    def _init():
        t_st[...] = jnp.zeros_like(t_st)
        s_st[...] = jnp.zeros_like(s_st)

    @pl.when(ph == 0)
    def _backward_states():
        jj = n_chunks - 1 - j
        for p in range(RET_PAIRS):
            sl = slice(p * c, (p + 1) * c)
            t_old = t_st[p]
            t_all[jj, p] = t_old.astype(BF16)
            ktd = (kt_ref[sl, :].astype(F32) * kdb[sl, :]).astype(BF16)
            kv = jnp.dot(ktd, v_ref[:, sl], preferred_element_type=F32)
            t_st[p] = t_old * cdb[sl, :] + jnp.where(same_head, kv, 0.0)

    @pl.when(ph == 1)
    def _forward():
        gmat = jnp.where(same_head, 1.0 / RET_HEAD_DIM, 0.0).astype(BF16)
        for p in range(RET_PAIRS):
            sl = slice(p * c, (p + 1) * c)
            qp = q_ref[:, sl]
            ktp = kt_ref[sl, :]
            vp = v_ref[:, sl]
            zq = jnp.zeros_like(qp)
            q2 = jnp.concatenate([jnp.where(lo, qp, zq), jnp.where(lo, zq, qp)], axis=0)
            s = jnp.dot(q2, ktp, preferred_element_type=F32)
            sd = (s * dcomb[2 * p * c:2 * (p + 1) * c, :]).astype(BF16)
            v2 = jnp.concatenate([jnp.where(lo, vp, zq), jnp.where(lo, zq, vp)], axis=0)
            inner = jnp.dot(jnp.concatenate([sd[0:c], sd[c:2 * c]], axis=1), v2,
                            preferred_element_type=F32)
            qf = qp.astype(F32)
            qdec = jnp.concatenate([(qf * qdf[:, sl]).astype(BF16), (qf * qdb[:, sl]).astype(BF16)], axis=1)
            s_old = s_st[p]
            st = jnp.concatenate([s_old.astype(BF16), t_all[j, p]], axis=0)
            o = inner + jnp.dot(qdec, st, preferred_element_type=F32)
            ktd = (ktp.astype(F32) * kdf[sl, :]).astype(BF16)
            kv = jnp.dot(ktd, vp, preferred_element_type=F32)
            s_st[p] = s_old * cdf[sl, :] + jnp.where(same_head, kv, 0.0)
            o2 = o * o
            o2h = o2.astype(BF16)
            o2l = (o2 - o2h.astype(F32)).astype(BF16)
            gms = (jnp.dot(o2h, gmat, preferred_element_type=F32)
                   + jnp.dot(o2l, gmat, preferred_element_type=F32))
            r = o * lax.rsqrt(gms + EPS) * gnw_ref[0:1, :]
            g = g_ref[:, sl]
            o_ref[:, sl] = (g * jax.nn.sigmoid(g) * r).astype(BF16)


def _retention(lg_lane, lg_row, lg_sc, gnw_b, rq, rkt, rv, rg, layer, batch, seq):
    n = seq // CHUNK
    c = CHUNK

    def tok_map(b, ph, j):
        return (b * n + jnp.where(ph == 0, n - 1 - j, j), 0)

    def feat_map(b, ph, j):
        return (b, 0, jnp.where(ph == 0, n - 1 - j, j))

    def fwd_only_map(b, ph, j):
        return (b * n + jnp.where(ph == 0, 0, j), 0)

    return pl.pallas_call(
        _ret_kernel,
        grid=(batch, 2, n),
        in_specs=[
            pl.BlockSpec((None, 2, 8, RET_WIDTH), lambda b, ph, j: (layer, 0, 0, 0)),
            pl.BlockSpec((None, 2, RET_WIDTH, c), lambda b, ph, j: (layer, 0, 0, 0)),
            pl.BlockSpec((None, 2, RET_HEADS * c, c), lambda b, ph, j: (layer, 0, 0, 0)),
            pl.BlockSpec((None, 8, LANES), lambda b, ph, j: (layer, 0, 0)),
            pl.BlockSpec((c, RET_WIDTH), fwd_only_map),
            pl.BlockSpec((None, RET_WIDTH, c), feat_map),
            pl.BlockSpec((c, RET_WIDTH), tok_map),
            pl.BlockSpec((c, RET_WIDTH), fwd_only_map),
        ],
        out_specs=pl.BlockSpec((c, RET_WIDTH), fwd_only_map),
        out_shape=jax.ShapeDtypeStruct((batch * seq, RET_WIDTH), BF16),
        scratch_shapes=[
            pltpu.VMEM((c, RET_WIDTH), F32), pltpu.VMEM((c, RET_WIDTH), F32),
            pltpu.VMEM((RET_WIDTH, c), F32), pltpu.VMEM((RET_WIDTH, c), F32),
            pltpu.VMEM((RET_WIDTH, c), F32), pltpu.VMEM((RET_WIDTH, c), F32),
            pltpu.VMEM((RET_HEADS * c, c), F32),
            pltpu.VMEM((RET_PAIRS, c, c), F32), pltpu.VMEM((RET_PAIRS, c, c), F32),
            pltpu.VMEM((n, RET_PAIRS, c, c), BF16),
        ],
        compiler_params=_cparams(3), name="retention",
    )(lg_lane, lg_row, lg_sc, gnw_b, rq, rkt, rv, rg)


def _post_kernel(x_ref, a_ref, r_ref, p_ref, wo_ref, ln2_ref, w1_ref, w2_ref, wg_ref, wp_ref, o_ref):
    x = x_ref[...]
    x = x + jnp.dot(a_ref[...], wo_ref[0:DIFF_WIDTH, :], preferred_element_type=F32)
    x = x + jnp.dot(r_ref[...], wo_ref[DIFF_WIDTH:, :], preferred_element_type=F32)
    ms = jnp.mean(x * x, axis=-1, keepdims=True)
    h2 = (x * lax.rsqrt(ms + EPS) * ln2_ref[...]).astype(BF16)
    hid = jnp.dot(h2, w1_ref[...], preferred_element_type=F32)
    act = jnp.square(jnp.maximum(hid, 0.0)).astype(BF16)
    x = x + jnp.dot(act, w2_ref[...], preferred_element_type=F32)
    gate = jax.nn.sigmoid(jnp.dot(x.astype(BF16), wg_ref[...], preferred_element_type=F32))
    pe = jnp.dot(p_ref[...].astype(BF16), wp_ref[...], preferred_element_type=F32)
    o_ref[...] = x + gate * pe


def _post(x, a, r, p, wo, ln2, w1, w2, wg, wp, layer, batch, seq):
    t = batch * seq
    tm = min(TOKEN_TILE, seq)
    lay3 = lambda i: (layer, 0, 0)

    def wspec(shape):
        return pl.BlockSpec((None,) + shape, lay3, pipeline_mode=pl.Buffered(1))

    return pl.pallas_call(
        _post_kernel,
        grid=(t // tm,),
        in_specs=[
            pl.BlockSpec((tm, D_MODEL), lambda i: (i, 0)),
            pl.BlockSpec((tm, DIFF_WIDTH), lambda i: (i, 0)),
            pl.BlockSpec((tm, RET_WIDTH), lambda i: (i, 0)),
            pl.BlockSpec((None, tm, PLE_DIM), lambda i: (layer, i, 0)),
            wspec((D_MODEL, D_MODEL)),
            pl.BlockSpec((None, 1, D_MODEL), lay3),
            wspec((D_MODEL, D_FF)),
            wspec((D_FF, D_MODEL)),
            wspec((D_MODEL, D_MODEL)),
            wspec((PLE_DIM, D_MODEL)),
        ],
        out_specs=pl.BlockSpec((tm, D_MODEL), lambda i: (i, 0)),
        out_shape=jax.ShapeDtypeStruct((t, D_MODEL), F32),
        compiler_params=_cparams(1), name="post",
    )(x, a, r, p, wo, ln2, w1, w2, wg, wp)


def _prepare_params(ln1_w, w_in, diff_q_norm, diff_k_norm, diff_subln, ret_decay_logit, ret_gn,
                    w_out, ln2_w, w_mlp1, w_mlp2, w_ple_gate, w_ple_proj):
    depth = w_in.shape[0]
    tm = TOKEN_TILE
    dq, dk, dv = w_in[:, :, 0:512], w_in[:, :, 512:1024], w_in[:, :, 1024:1536]
    rq, rk = w_in[:, :, 1536:2048], w_in[:, :, 2048:2560]
    rv, rg = w_in[:, :, 2560:3072], w_in[:, :, 3072:3584]
    wf = jnp.swapaxes(jnp.concatenate([dq, dk, dv, rk], axis=2), 1, 2).astype(BF16)
    wt = jnp.concatenate([rq, rv, rg], axis=2).astype(BF16)
    q_scale = (QK_DIM ** -0.5) * LOG2E
    qw_b = jnp.broadcast_to((diff_q_norm.astype(F32) * q_scale)[:, :, None], (depth, QK_DIM, tm))
    kw_b = jnp.broadcast_to(diff_k_norm.astype(F32)[:, :, None], (depth, QK_DIM, tm))
    lam_init = [0.8 - 0.6 * math.exp(-0.3 * i) for i in range(depth)]
    sub_scale = jnp.asarray([1.0 - li for li in lam_init], F32)[:, None]
    sw_b = jnp.broadcast_to((diff_subln.astype(F32) * sub_scale)[:, :, None], (depth, DV_DIM, Q_TILE))
    lgt = ret_decay_logit.astype(F32)
    lg_lane = jnp.broadcast_to(jnp.repeat(lgt, RET_HEAD_DIM, axis=2)[:, :, None, :], (depth, 2, 8, RET_WIDTH))
    lg_row = jnp.broadcast_to(jnp.repeat(lgt, RET_HEAD_DIM, axis=2)[:, :, :, None], (depth, 2, RET_WIDTH, CHUNK))
    lg_sc = jnp.broadcast_to(jnp.repeat(lgt, CHUNK, axis=2)[:, :, :, None], (depth, 2, RET_HEADS * CHUNK, CHUNK))
    gnw_b = jnp.broadcast_to(jnp.tile(ret_gn.astype(F32), (1, 2))[:, None, :], (depth, 8, LANES))
    return dict(
        ln1=ln1_w.astype(F32)[:, None, :], wf=wf, wt=wt, qw_b=qw_b, kw_b=kw_b, sw_b=sw_b, lam_init=lam_init,
        qw=(diff_q_norm.astype(F32) * q_scale)[:, None, :], kw=diff_k_norm.astype(F32)[:, None, :],
        lg_lane=lg_lane, lg_row=lg_row, lg_sc=lg_sc, gnw_b=gnw_b,
        wo=w_out.astype(BF16), ln2=ln2_w.astype(F32)[:, None, :], w1=w_mlp1.astype(BF16),
        w2=w_mlp2.astype(BF16), wg=w_ple_gate.astype(BF16), wp=w_ple_proj.astype(BF16))


def _run_trunk(x, p, prm, tabs, diff_lambda):
    batch, seq, _ = x.shape
    depth = p.shape[0]
    xf = x.reshape(batch * seq, D_MODEL)
    pf = p.reshape(depth, batch * seq, PLE_DIM)
    lp = diff_lambda.astype(F32)
    for i in range(depth):
        qt, k, v, rq, rkt, rv, rg = _inproj(xf, prm["ln1"], prm["wf"], prm["wt"], prm["qw_b"], prm["kw_b"],
                                            tabs, i, batch, seq)
        a = _diff_attention(lp, prm["qw"], prm["kw"], qt, k, v, prm["sw_b"], i, prm["lam_init"][i],
                            batch, seq)
        r = _retention(prm["lg_lane"], prm["lg_row"], prm["lg_sc"], prm["gnw_b"], rq, rkt, rv, rg,
                       i, batch, seq)
        xf = _post(xf, a.reshape(batch * seq, DIFF_WIDTH), r, pf, prm["wo"], prm["ln2"], prm["w1"],
                   prm["w2"], prm["wg"], prm["wp"], i, batch, seq)
    return xf.reshape(batch, seq, D_MODEL)


def kernel(x_prompt, x_sample, p_prompt, p_sample, ln1_w, w_in, diff_q_norm, diff_k_norm, diff_lambda,
           diff_subln, ret_decay_logit, ret_gn, w_out, ln2_w, w_mlp1, w_mlp2, w_ple_gate, w_ple_proj):
    prm = _prepare_params(ln1_w, w_in, diff_q_norm, diff_k_norm, diff_subln, ret_decay_logit, ret_gn,
                          w_out, ln2_w, w_mlp1, w_mlp2, w_ple_gate, w_ple_proj)
    tabs = _rope_tables(max(x_prompt.shape[1], x_sample.shape[1]))
    y_prompt = _run_trunk(x_prompt, p_prompt, prm, tabs, diff_lambda)
    y_sample = _run_trunk(x_sample, p_sample, prm, tabs, diff_lambda)
    return (y_prompt, y_sample)
```

```python
import functools
import math

import jax
import jax.numpy as jnp
from jax import lax
from jax.experimental import pallas as pl
from jax.experimental.pallas import tpu as pltpu

F32 = jnp.float32
BF16 = jnp.bfloat16

D_MODEL = 1024
DIFF_WIDTH = 512
RET_WIDTH = 512
QK_DIM = 64
DV_DIM = 128
DIFF_HEADS = 4
RET_HEAD_DIM = 64
RET_HEADS = 8
RET_PAIRS = RET_HEADS // 2
ROT_DIMS = 16
ROPE_THETA = 500000.0
RET_THETA = 10000.0
D_FF = 4096
PLE_DIM = 256
CHUNK = 128
EPS = 1e-6
LOG2E = 1.4426950408889634
NEG_BIG = -1e30

VMEM_LIMIT_BYTES = 56 * 1024 * 1024
LANES = 128
V_ROWS = DV_DIM + 16

TOKEN_TILE = 512
Q_TILE = 1024
KV_TILE = 256
KV_TILE_FAST = 2048
BOUND_SLACK = 1.02
UNDERFLOW_GUARD = 2.0 ** -64
RET_GROUP = 4
TABLE_TILE = 1024


def _cparams(n_grid):
    return pltpu.CompilerParams(dimension_semantics=("arbitrary",) * n_grid,
                                vmem_limit_bytes=VMEM_LIMIT_BYTES)


def _const_spec(shape):
    nd = len(shape)
    return pl.BlockSpec(shape, lambda *_: (0,) * nd)


def _rope_kernel(invd_ref, invrt_ref, invr_ref, sgn_ref,
                 cosd_ref, sind_ref, cosrt_ref, sinrt_ref, cosr_ref, sinr_ref):
    ts = cosd_ref.shape[1]
    base = pl.program_id(0) * ts
    pos_d = (base + lax.broadcasted_iota(jnp.int32, (8, ts), 1)).astype(F32)
    ang_d = pos_d * invd_ref[...]
    cosd_ref[...] = jnp.cos(ang_d)
    sind_ref[...] = jnp.sin(ang_d)
    pos_r = (base + lax.broadcasted_iota(jnp.int32, (32, ts), 1)).astype(F32)
    ang_r = pos_r * invrt_ref[...]
    cosrt_ref[...] = jnp.cos(ang_r)
    sinrt_ref[...] = jnp.sin(ang_r)
    pos_t = (base + lax.broadcasted_iota(jnp.int32, (ts, LANES), 0)).astype(F32)
    ang_t = pos_t * invr_ref[0:1, :]
    cosr_ref[...] = jnp.cos(ang_t)
    sinr_ref[...] = jnp.sin(ang_t) * sgn_ref[0:1, :]


def _rope_tables(s_max):
    ts = min(TABLE_TILE, s_max)
    inv_d = 1.0 / (ROPE_THETA ** (jnp.arange(0, ROT_DIMS, 2, dtype=F32) / ROT_DIMS))
    inv_r = 1.0 / (RET_THETA ** (jnp.arange(0, RET_HEAD_DIM, 2, dtype=F32) / RET_HEAD_DIM))
    invd_b = jnp.broadcast_to(inv_d[:, None], (8, ts))
    invrt_b = jnp.broadcast_to(inv_r[:, None], (32, ts))
    invr_lane = jnp.broadcast_to(jnp.tile(inv_r, LANES // 32)[None, :], (8, LANES))
    sgn = jnp.where((jnp.arange(LANES) % RET_HEAD_DIM) < RET_HEAD_DIM // 2, -1.0, 1.0).astype(F32)
    sgn_b = jnp.broadcast_to(sgn[None, :], (8, LANES))
    return pl.pallas_call(
        _rope_kernel,
        grid=(s_max // ts,),
        in_specs=[_const_spec((8, ts)), _const_spec((32, ts)), _const_spec((8, LANES)),
                  _const_spec((8, LANES))],
        out_specs=[pl.BlockSpec((8, ts), lambda i: (0, i)), pl.BlockSpec((8, ts), lambda i: (0, i)),
                   pl.BlockSpec((32, ts), lambda i: (0, i)), pl.BlockSpec((32, ts), lambda i: (0, i)),
                   pl.BlockSpec((ts, LANES), lambda i: (i, 0)), pl.BlockSpec((ts, LANES), lambda i: (i, 0))],
        out_shape=[jax.ShapeDtypeStruct((8, s_max), F32), jax.ShapeDtypeStruct((8, s_max), F32),
                   jax.ShapeDtypeStruct((32, s_max), F32), jax.ShapeDtypeStruct((32, s_max), F32),
                   jax.ShapeDtypeStruct((s_max, LANES), F32), jax.ShapeDtypeStruct((s_max, LANES), F32)],
        compiler_params=_cparams(1),
        name="rope_tables",
    )(invd_b, invrt_b, invr_lane, sgn_b)


def _inproj_kernel(x_ref, ln_ref, wf_ref, wt_ref, qw_ref, kw_ref,
                   cd_ref, sd_ref, crt_ref, srt_ref, cr_ref, sr_ref,
                   qt_ref, k_ref, v_ref, rq_ref, rkt_ref, rv_ref, rg_ref):
    tm = x_ref.shape[0]
    x = x_ref[...]
    ms = jnp.mean(x * x, axis=-1, keepdims=True)
    hn = (x * lax.rsqrt(ms + EPS) * ln_ref[...]).astype(BF16)
    feat = lax.dot_general(wf_ref[...], hn, (((1,), (1,)), ((), ())), preferred_element_type=F32)
    tok = jnp.dot(hn, wt_ref[...], preferred_element_type=F32)

    cd = cd_ref[...]
    sd = sd_ref[...]

    def norm_rot(z, w):
        z3 = z.reshape(2 * DIFF_HEADS, QK_DIM, tm)
        gms = jnp.mean(z3 * z3, axis=1, keepdims=True)
        y = z3 * lax.rsqrt(gms + EPS) * w[None]
        y1 = y[:, 0:8]
        y2 = y[:, 8:16]
        out = jnp.concatenate([y1 * cd - y2 * sd, y1 * sd + y2 * cd, y[:, 16:]], axis=1)
        return out.reshape(DIFF_WIDTH, tm)

    qt_ref[...] = norm_rot(feat[0:512], qw_ref[...]).astype(BF16)
    k_ref[...] = norm_rot(feat[512:1024], kw_ref[...]).T.astype(BF16)

    v3 = feat[1024:1536].reshape(DIFF_HEADS, DV_DIM, tm).astype(BF16)
    v_ref[:, 0:DV_DIM, :] = v3
    v_ref[:, DV_DIM:V_ROWS, :] = jnp.ones((DIFF_HEADS, V_ROWS - DV_DIM, tm), BF16)

    half = RET_HEAD_DIM // 2
    rk3 = feat[1536:2048].reshape(RET_HEADS, RET_HEAD_DIM, tm)
    crt = crt_ref[...]
    srt = srt_ref[...]
    r1 = rk3[:, 0:half]
    r2 = rk3[:, half:]
    rk_rot = jnp.concatenate([r1 * crt - r2 * srt, r1 * srt + r2 * crt], axis=1)
    rkt_ref[...] = (rk_rot * (RET_HEAD_DIM ** -0.5)).reshape(RET_WIDTH, tm).astype(BF16)

    rq = tok[:, 0:512]
    lane = lax.broadcasted_iota(jnp.int32, rq.shape, 1)
    partner = jnp.where((lane % RET_HEAD_DIM) < half,
                        pltpu.roll(rq, RET_WIDTH - half, 1), pltpu.roll(rq, half, 1))
    cr = jnp.concatenate([cr_ref[...]] * (RET_WIDTH // LANES), axis=1)
    sr = jnp.concatenate([sr_ref[...]] * (RET_WIDTH // LANES), axis=1)
    rq_ref[...] = (rq * cr + partner * sr).astype(BF16)
    rv_ref[...] = tok[:, 512:1024].astype(BF16)
    rg_ref[...] = tok[:, 1024:1536]


def _inproj(x, ln1, wf, wt, qw_b, kw_b, tabs, layer, batch, seq):
    t = batch * seq
    tm = min(TOKEN_TILE, seq)
    nst = seq // tm
    cosd, sind, cosrt, sinrt, cosr, sinr = tabs
    lay3 = lambda i: (layer, 0, 0)
    in_specs = [
        pl.BlockSpec((tm, D_MODEL), lambda i: (i, 0)),
        pl.BlockSpec((None, 1, D_MODEL), lay3),
        pl.BlockSpec((None, 2048, D_MODEL), lay3),
        pl.BlockSpec((None, D_MODEL, 1536), lay3),
        pl.BlockSpec((None, QK_DIM, tm), lay3),
        pl.BlockSpec((None, QK_DIM, tm), lay3),
        pl.BlockSpec((8, tm), lambda i: (0, i % nst)),
        pl.BlockSpec((8, tm), lambda i: (0, i % nst)),
        pl.BlockSpec((32, tm), lambda i: (0, i % nst)),
        pl.BlockSpec((32, tm), lambda i: (0, i % nst)),
        pl.BlockSpec((tm, LANES), lambda i: (i % nst, 0)),
        pl.BlockSpec((tm, LANES), lambda i: (i % nst, 0)),
    ]
    feat_spec = pl.BlockSpec((None, 512, tm), lambda i: (i // nst, 0, i % nst))
    tok_spec = pl.BlockSpec((tm, 512), lambda i: (i, 0))
    out_specs = [
        feat_spec,
        tok_spec,
        pl.BlockSpec((None, DIFF_HEADS, V_ROWS, tm), lambda i: (i // nst, 0, 0, i % nst)),
        tok_spec,
        feat_spec,
        tok_spec,
        tok_spec,
    ]
    out_shape = [
        jax.ShapeDtypeStruct((batch, 512, seq), BF16),
        jax.ShapeDtypeStruct((t, 512), BF16),
        jax.ShapeDtypeStruct((batch, DIFF_HEADS, V_ROWS, seq), BF16),
        jax.ShapeDtypeStruct((t, 512), BF16),
        jax.ShapeDtypeStruct((batch, 512, seq), BF16),
        jax.ShapeDtypeStruct((t, 512), BF16),
        jax.ShapeDtypeStruct((t, 512), F32),
    ]
    return pl.pallas_call(
        _inproj_kernel, grid=(t // tm,), in_specs=in_specs, out_specs=out_specs, out_shape=out_shape,
        compiler_params=_cparams(1), name="inproj",
    )(x, ln1, wf, wt, qw_b, kw_b, cosd, sind, cosrt, sinrt, cosr, sinr)


def _attn_kernel(lp_ref, qw_ref, kw_ref, qt_ref, k_ref, v_ref, sw_ref, o_ref, acc1, acc2, m1, m2,
                 *, lam_init, tk, tk_fast):
    seq = k_ref.shape[0]
    qt = qt_ref[...]
    row = lax.broadcasted_iota(jnp.int32, qt.shape, 0)
    zero = jnp.zeros_like(qt)
    qa = jnp.where(row < QK_DIM, qt, zero)
    qb = jnp.where(row >= QK_DIM, qt, zero)
    comps = ((qa, acc1, m1), (qb, acc2, m2))

    bound = (jnp.max(jnp.abs(qw_ref[...]), axis=1, keepdims=True)
             * jnp.max(jnp.abs(kw_ref[...]), axis=1, keepdims=True) * (QK_DIM * BOUND_SLACK))
    acc1[...] = jnp.zeros_like(acc1)
    acc2[...] = jnp.zeros_like(acc2)

    def bound_body(j, carry):
        off = pl.multiple_of(j * tk_fast, tk_fast)
        kb = k_ref[pl.ds(off, tk_fast), :]
        vb = v_ref[:, pl.ds(off, tk_fast)]
        for qx, acc, _ in comps:
            s = jnp.dot(kb, qx, preferred_element_type=F32)
            p = jnp.exp2(s - bound).astype(BF16)
            acc[...] += jnp.dot(vb, p, preferred_element_type=F32)
        return carry

    lax.fori_loop(0, seq // tk_fast, bound_body, 0)
    l_min = jnp.minimum(jnp.min(acc1[DV_DIM:DV_DIM + 1, :]), jnp.min(acc2[DV_DIM:DV_DIM + 1, :]))

    @pl.when(l_min < UNDERFLOW_GUARD)
    def _running_max_pass():
        acc1[...] = jnp.zeros_like(acc1)
        acc2[...] = jnp.zeros_like(acc2)
        m1[...] = jnp.full_like(m1, NEG_BIG)
        m2[...] = jnp.full_like(m2, NEG_BIG)

        def body(j, carry):
            off = pl.multiple_of(j * tk, tk)
            kb = k_ref[pl.ds(off, tk), :]
            vb = v_ref[:, pl.ds(off, tk)]
            for qx, acc, m in comps:
                s = jnp.dot(kb, qx, preferred_element_type=F32)
                m_old = m[...]
                m_new = jnp.maximum(m_old, jnp.max(s, axis=0, keepdims=True))
                p = jnp.exp2(s - m_new).astype(BF16)
                alpha = jnp.exp2(m_old - m_new)
                acc[...] = acc[...] * alpha + jnp.dot(vb, p, preferred_element_type=F32)
                m[...] = m_new
            return carry

        lax.fori_loop(0, seq // tk, body, 0)

    lp = lp_ref[...]
    lam = (jnp.exp(jnp.sum(lp[0:1] * lp[1:2], axis=1, keepdims=True))
           - jnp.exp(jnp.sum(lp[2:3] * lp[3:4], axis=1, keepdims=True)) + lam_init)
    a1 = acc1[...]
    a2 = acc2[...]
    n1 = a1[0:DV_DIM] / a1[DV_DIM:DV_DIM + 1]
    n2 = a2[0:DV_DIM] / a2[DV_DIM:DV_DIM + 1]
    out = n1 - lam * n2
    oms = jnp.mean(out * out, axis=0, keepdims=True)
    y = out * lax.rsqrt(oms + EPS) * sw_ref[...]
    o_ref[...] = y.T.astype(BF16)


def _diff_attention(lp, qw, kw, qt, k, v, sw_b, layer, lam_init, batch, seq):
    tq = min(Q_TILE, seq)
    tk = min(KV_TILE, seq)
    tk_fast = min(KV_TILE_FAST, seq)
    k3 = k.reshape(batch, seq, DIFF_WIDTH)
    kern = functools.partial(_attn_kernel, lam_init=lam_init, tk=tk, tk_fast=tk_fast)
    lay3 = lambda b, h, i: (layer, 0, 0)
    return pl.pallas_call(
        kern,
        grid=(batch, DIFF_HEADS, seq // tq),
        in_specs=[
            pl.BlockSpec((None, 4, QK_DIM), lay3),
            pl.BlockSpec((None, 1, QK_DIM), lay3),
            pl.BlockSpec((None, 1, QK_DIM), lay3),
            pl.BlockSpec((None, DV_DIM, tq), lambda b, h, i: (b, h, i)),
            pl.BlockSpec((None, seq, DV_DIM), lambda b, h, i: (b, 0, h)),
            pl.BlockSpec((None, None, V_ROWS, seq), lambda b, h, i: (b, h, 0, 0)),
            pl.BlockSpec((None, DV_DIM, tq), lay3),
        ],
        out_specs=pl.BlockSpec((None, tq, DV_DIM), lambda b, h, i: (b, i, h)),
        out_shape=jax.ShapeDtypeStruct((batch, seq, DIFF_WIDTH), BF16),
        scratch_shapes=[pltpu.VMEM((V_ROWS, tq), F32), pltpu.VMEM((V_ROWS, tq), F32),
                        pltpu.VMEM((1, tq), F32), pltpu.VMEM((1, tq), F32)],
        compiler_params=_cparams(3), name="diff_attention",
    )(lp, qw, kw, qt, k3, v, sw_b)


def _ret_kernel(lgl_ref, lgr_ref, lgs_ref, gnw_ref, q_ref, kt_ref, v_ref, g_ref, o_ref,
                qdf, qdb, kdf, kdb, cdf, cdb, dcomb, gfull, s_st, t_st, t_all, o_buf, *, group):
    b = pl.program_id(0)
    ph = pl.program_id(1)
    j = pl.program_id(2)
    n_groups = pl.num_programs(2)
    c = CHUNK

    @pl.when((b == 0) & (ph == 0) & (j == 0))
    def _tables():
        lgl = jax.nn.log_sigmoid(lgl_ref[...])
        lgr = jax.nn.log_sigmoid(lgr_ref[...])
        lgs = jax.nn.log_sigmoid(lgs_ref[...])
        pos_row = lax.broadcasted_iota(jnp.int32, (c, RET_WIDTH), 0).astype(F32)
        qdf[...] = jnp.exp((pos_row + 1.0) * lgl[0, 0:1, :])
        qdb[...] = jnp.exp((c - pos_row) * lgl[1, 0:1, :])
        pos_lane = lax.broadcasted_iota(jnp.int32, (RET_WIDTH, c), 1).astype(F32)
        kdf[...] = jnp.exp((c - 1.0 - pos_lane) * lgr[0])
        kdb[...] = jnp.exp(pos_lane * lgr[1])
        cdf[...] = jnp.exp(float(c) * lgr[0])
        cdb[...] = jnp.exp(float(c) * lgr[1])
        qi = (lax.broadcasted_iota(jnp.int32, (RET_HEADS * c, c), 0) % c).astype(F32)
        ki = lax.broadcasted_iota(jnp.int32, (RET_HEADS * c, c), 1).astype(F32)
        rel = qi - ki
        dcomb[...] = jnp.where(rel >= 0, jnp.exp(rel * lgs[0]), jnp.exp(-rel * lgs[1]))

        frow = lax.broadcasted_iota(jnp.int32, (RET_WIDTH, RET_WIDTH), 0) // RET_HEAD_DIM
        fcol = lax.broadcasted_iota(jnp.int32, (RET_WIDTH, RET_WIDTH), 1) // RET_HEAD_DIM
        gfull[...] = jnp.where(frow == fcol, 1.0 / RET_HEAD_DIM, 0.0).astype(BF16)

    lane = lax.broadcasted_iota(jnp.int32, (c, c), 1)
    row = lax.broadcasted_iota(jnp.int32, (c, c), 0)
    lo = lane < RET_HEAD_DIM
    same_head = (row < RET_HEAD_DIM) == lo

    @pl.when((ph == 0) & (j == 0))
    def _init():
        t_st[...] = jnp.zeros_like(t_st)
        s_st[...] = jnp.zeros_like(s_st)

    @pl.when(ph == 0)
    def _backward_states():
        first = (n_groups - 1 - j) * group
        for p in range(RET_PAIRS):
            sl = slice(p * c, (p + 1) * c)
            t = t_st[p]
            for g in reversed(range(group)):
                gs = slice(g * c, (g + 1) * c)
                t_all[first + g, p] = t.astype(BF16)
                ktd = (kt_ref[sl, gs].astype(F32) * kdb[sl, :]).astype(BF16)
                kv = jnp.dot(ktd, v_ref[gs, sl], preferred_element_type=F32)
                t = t * cdb[sl, :] + jnp.where(same_head, kv, 0.0)
            t_st[p] = t

    @pl.when(ph == 1)
    def _forward():
        first = j * group
        combos = [(p, g) for p in range(RET_PAIRS) for g in range(group)]
        zq = jnp.zeros((c, c), BF16)

        def blk(p, g):
            return slice(g * c, (g + 1) * c), slice(p * c, (p + 1) * c)

        kvs, sds = {}, {}
        for p, g in combos:
            gs, sl = blk(p, g)
            ktp = kt_ref[sl, gs]
            ktd = (ktp.astype(F32) * kdf[sl, :]).astype(BF16)
            kvs[p, g] = jnp.dot(ktd, v_ref[gs, sl], preferred_element_type=F32)
            qp = q_ref[gs, sl]
            q2 = jnp.concatenate([jnp.where(lo, qp, zq), jnp.where(lo, zq, qp)], axis=0)
            s = jnp.dot(q2, ktp, preferred_element_type=F32)
            sds[p, g] = (s * dcomb[2 * p * c:2 * (p + 1) * c, :]).astype(BF16)
        states = {}
        for p in range(RET_PAIRS):
            sl = slice(p * c, (p + 1) * c)
            s_state = s_st[p]
            for g in range(group):
                states[p, g] = s_state.astype(BF16)
                s_state = s_state * cdf[sl, :] + jnp.where(same_head, kvs[p, g], 0.0)
            s_st[p] = s_state
        for p, g in combos:
            gs, sl = blk(p, g)
            qp = q_ref[gs, sl]
            vp = v_ref[gs, sl]
            qf = qp.astype(F32)
            sd = sds[p, g]
            lhs = jnp.concatenate([sd[0:c], sd[c:2 * c], (qf * qdf[:, sl]).astype(BF16),
                                   (qf * qdb[:, sl]).astype(BF16)], axis=1)
            rhs = jnp.concatenate([jnp.where(lo, vp, zq), jnp.where(lo, zq, vp),
                                   states[p, g], t_all[first + g, p]], axis=0)
            o_buf[gs, sl] = jnp.dot(lhs, rhs, preferred_element_type=F32)
        o = o_buf[...]
        gms = jnp.dot((o * o).astype(BF16), gfull[...], preferred_element_type=F32)
        r = o * lax.rsqrt(gms + EPS) * gnw_ref[0:1, :]
        gate = g_ref[...]
        o_ref[...] = (gate * jax.nn.sigmoid(gate) * r).astype(BF16)


def _retention(lg_lane, lg_row, lg_sc, gnw_b, rq, rkt, rv, rg, layer, batch, seq):
    c = CHUNK
    n = seq // c
    group = min(RET_GROUP, n)
    ng = n // group
    tg = group * c

    def tok_map(b, ph, j):
        return (b * ng + jnp.where(ph == 0, ng - 1 - j, j), 0)

    def feat_map(b, ph, j):
        return (b, 0, jnp.where(ph == 0, ng - 1 - j, j))

    def fwd_only_map(b, ph, j):
        return (b * ng + jnp.where(ph == 0, 0, j), 0)

    return pl.pallas_call(
        functools.partial(_ret_kernel, group=group),
        grid=(batch, 2, ng),
        in_specs=[
            pl.BlockSpec((None, 2, 8, RET_WIDTH), lambda b, ph, j: (layer, 0, 0, 0)),
            pl.BlockSpec((None, 2, RET_WIDTH, c), lambda b, ph, j: (layer, 0, 0, 0)),
            pl.BlockSpec((None, 2, RET_HEADS * c, c), lambda b, ph, j: (layer, 0, 0, 0)),
            pl.BlockSpec((None, 8, RET_WIDTH), lambda b, ph, j: (layer, 0, 0)),
            pl.BlockSpec((tg, RET_WIDTH), fwd_only_map),
            pl.BlockSpec((None, RET_WIDTH, tg), feat_map),
            pl.BlockSpec((tg, RET_WIDTH), tok_map),
            pl.BlockSpec((tg, RET_WIDTH), fwd_only_map),
        ],
        out_specs=pl.BlockSpec((tg, RET_WIDTH), fwd_only_map),
        out_shape=jax.ShapeDtypeStruct((batch * seq, RET_WIDTH), BF16),
        scratch_shapes=[
            pltpu.VMEM((c, RET_WIDTH), F32), pltpu.VMEM((c, RET_WIDTH), F32),
            pltpu.VMEM((RET_WIDTH, c), F32), pltpu.VMEM((RET_WIDTH, c), F32),
            pltpu.VMEM((RET_WIDTH, c), F32), pltpu.VMEM((RET_WIDTH, c), F32),
            pltpu.VMEM((RET_HEADS * c, c), F32),
            pltpu.VMEM((RET_WIDTH, RET_WIDTH), BF16),
            pltpu.VMEM((RET_PAIRS, c, c), F32), pltpu.VMEM((RET_PAIRS, c, c), F32),
            pltpu.VMEM((n, RET_PAIRS, c, c), BF16),
            pltpu.VMEM((tg, RET_WIDTH), F32),
        ],
        compiler_params=_cparams(3), name="retention",
    )(lg_lane, lg_row, lg_sc, gnw_b, rq, rkt, rv, rg)


def _post_kernel(x_ref, a_ref, r_ref, p_ref, wo_ref, ln2_ref, w1_ref, w2_ref, wg_ref, wp_ref, o_ref):
    x = x_ref[...]
    x = x + jnp.dot(a_ref[...], wo_ref[0:DIFF_WIDTH, :], preferred_element_type=F32)
    x = x + jnp.dot(r_ref[...], wo_ref[DIFF_WIDTH:, :], preferred_element_type=F32)
    ms = jnp.mean(x * x, axis=-1, keepdims=True)
    h2 = (x * lax.rsqrt(ms + EPS) * ln2_ref[...]).astype(BF16)
    hid = jnp.dot(h2, w1_ref[...], preferred_element_type=F32)
    act = jnp.square(jnp.maximum(hid, 0.0)).astype(BF16)
    x = x + jnp.dot(act, w2_ref[...], preferred_element_type=F32)
    gate = jax.nn.sigmoid(jnp.dot(x.astype(BF16), wg_ref[...], preferred_element_type=F32))
    pe = jnp.dot(p_ref[...].astype(BF16), wp_ref[...], preferred_element_type=F32)
    o_ref[...] = x + gate * pe


def _post(x, a, r, p, wo, ln2, w1, w2, wg, wp, layer, batch, seq):
    t = batch * seq
    tm = min(TOKEN_TILE, seq)
    lay3 = lambda i: (layer, 0, 0)

    def wspec(shape):
        return pl.BlockSpec((None,) + shape, lay3, pipeline_mode=pl.Buffered(1))

    return pl.pallas_call(
        _post_kernel,
        grid=(t // tm,),
        in_specs=[
            pl.BlockSpec((tm, D_MODEL), lambda i: (i, 0)),
            pl.BlockSpec((tm, DIFF_WIDTH), lambda i: (i, 0)),
            pl.BlockSpec((tm, RET_WIDTH), lambda i: (i, 0)),
            pl.BlockSpec((None, tm, PLE_DIM), lambda i: (layer, i, 0)),
            wspec((D_MODEL, D_MODEL)),
            pl.BlockSpec((None, 1, D_MODEL), lay3),
            wspec((D_MODEL, D_FF)),
            wspec((D_FF, D_MODEL)),
            wspec((D_MODEL, D_MODEL)),
            wspec((PLE_DIM, D_MODEL)),
        ],
        out_specs=pl.BlockSpec((tm, D_MODEL), lambda i: (i, 0)),
        out_shape=jax.ShapeDtypeStruct((t, D_MODEL), F32),
        compiler_params=_cparams(1), name="post",
    )(x, a, r, p, wo, ln2, w1, w2, wg, wp)


def _prepare_params(ln1_w, w_in, diff_q_norm, diff_k_norm, diff_subln, ret_decay_logit, ret_gn,
                    w_out, ln2_w, w_mlp1, w_mlp2, w_ple_gate, w_ple_proj):
    depth = w_in.shape[0]
    tm = TOKEN_TILE
    dq, dk, dv = w_in[:, :, 0:512], w_in[:, :, 512:1024], w_in[:, :, 1024:1536]
    rq, rk = w_in[:, :, 1536:2048], w_in[:, :, 2048:2560]
    rv, rg = w_in[:, :, 2560:3072], w_in[:, :, 3072:3584]
    wf = jnp.swapaxes(jnp.concatenate([dq, dk, dv, rk], axis=2), 1, 2).astype(BF16)
    wt = jnp.concatenate([rq, rv, rg], axis=2).astype(BF16)
    q_scale = (QK_DIM ** -0.5) * LOG2E
    qw_b = jnp.broadcast_to((diff_q_norm.astype(F32) * q_scale)[:, :, None], (depth, QK_DIM, tm))
    kw_b = jnp.broadcast_to(diff_k_norm.astype(F32)[:, :, None], (depth, QK_DIM, tm))
    lam_init = [0.8 - 0.6 * math.exp(-0.3 * i) for i in range(depth)]
    sub_scale = jnp.asarray([1.0 - li for li in lam_init], F32)[:, None]
    sw_b = jnp.broadcast_to((diff_subln.astype(F32) * sub_scale)[:, :, None], (depth, DV_DIM, Q_TILE))
    lgt = ret_decay_logit.astype(F32)
    lg_lane = jnp.broadcast_to(jnp.repeat(lgt, RET_HEAD_DIM, axis=2)[:, :, None, :], (depth, 2, 8, RET_WIDTH))
    lg_row = jnp.broadcast_to(jnp.repeat(lgt, RET_HEAD_DIM, axis=2)[:, :, :, None], (depth, 2, RET_WIDTH, CHUNK))
    lg_sc = jnp.broadcast_to(jnp.repeat(lgt, CHUNK, axis=2)[:, :, :, None], (depth, 2, RET_HEADS * CHUNK, CHUNK))
    gnw_b = jnp.broadcast_to(jnp.tile(ret_gn.astype(F32), (1, RET_HEADS))[:, None, :], (depth, 8, RET_WIDTH))
    return dict(
        ln1=ln1_w.astype(F32)[:, None, :], wf=wf, wt=wt, qw_b=qw_b, kw_b=kw_b, sw_b=sw_b, lam_init=lam_init,
        qw=(diff_q_norm.astype(F32) * q_scale)[:, None, :], kw=diff_k_norm.astype(F32)[:, None, :],
        lg_lane=lg_lane, lg_row=lg_row, lg_sc=lg_sc, gnw_b=gnw_b,
        wo=w_out.astype(BF16), ln2=ln2_w.astype(F32)[:, None, :], w1=w_mlp1.astype(BF16),
        w2=w_mlp2.astype(BF16), wg=w_ple_gate.astype(BF16), wp=w_ple_proj.astype(BF16))


def _run_trunk(x, p, prm, tabs, diff_lambda):
    batch, seq, _ = x.shape
    depth = p.shape[0]
    xf = x.reshape(batch * seq, D_MODEL)
    pf = p.reshape(depth, batch * seq, PLE_DIM)
    lp = diff_lambda.astype(F32)
    for i in range(depth):
        qt, k, v, rq, rkt, rv, rg = _inproj(xf, prm["ln1"], prm["wf"], prm["wt"], prm["qw_b"], prm["kw_b"],
                                            tabs, i, batch, seq)
        a = _diff_attention(lp, prm["qw"], prm["kw"], qt, k, v, prm["sw_b"], i, prm["lam_init"][i],
                            batch, seq)
        r = _retention(prm["lg_lane"], prm["lg_row"], prm["lg_sc"], prm["gnw_b"], rq, rkt, rv, rg,
                       i, batch, seq)
        xf = _post(xf, a.reshape(batch * seq, DIFF_WIDTH), r, pf, prm["wo"], prm["ln2"], prm["w1"],
                   prm["w2"], prm["wg"], prm["wp"], i, batch, seq)
    return xf.reshape(batch, seq, D_MODEL)


def kernel(x_prompt, x_sample, p_prompt, p_sample, ln1_w, w_in, diff_q_norm, diff_k_norm, diff_lambda,
           diff_subln, ret_decay_logit, ret_gn, w_out, ln2_w, w_mlp1, w_mlp2, w_ple_gate, w_ple_proj):
    prm = _prepare_params(ln1_w, w_in, diff_q_norm, diff_k_norm, diff_subln, ret_decay_logit, ret_gn,
                          w_out, ln2_w, w_mlp1, w_mlp2, w_ple_gate, w_ple_proj)
    tabs = _rope_tables(max(x_prompt.shape[1], x_sample.shape[1]))
    y_prompt = _run_trunk(x_prompt, p_prompt, prm, tabs, diff_lambda)
    y_sample = _run_trunk(x_sample, p_sample, prm, tabs, diff_lambda)
    return (y_prompt, y_sample)
```

```python
import functools
import math

import jax
import jax.numpy as jnp
from jax import lax
from jax.experimental import pallas as pl
from jax.experimental.pallas import tpu as pltpu

F32 = jnp.float32
BF16 = jnp.bfloat16

D_MODEL = 1024
DIFF_WIDTH = 512
RET_WIDTH = 512
QK_DIM = 64
DV_DIM = 128
DIFF_HEADS = 4
RET_HEAD_DIM = 64
RET_HEADS = 8
RET_PAIRS = RET_HEADS // 2
ROT_DIMS = 16
ROPE_THETA = 500000.0
RET_THETA = 10000.0
D_FF = 4096
PLE_DIM = 256
CHUNK = 128
EPS = 1e-6
LOG2E = 1.4426950408889634
NEG_BIG = -1e30

VMEM_LIMIT_BYTES = 56 * 1024 * 1024
LANES = 128
V_ROWS = DV_DIM

TOKEN_TILE = 512
Q_TILE = 1024
KV_TILE = 256
KV_TILE_FAST = 4096
BOUND_SLACK = 1.02
UNDERFLOW_GUARD = 2.0 ** -64
RET_GROUP = 4
TABLE_TILE = 1024


def _cparams(n_grid):
    return pltpu.CompilerParams(dimension_semantics=("arbitrary",) * n_grid,
                                vmem_limit_bytes=VMEM_LIMIT_BYTES)


def _const_spec(shape):
    nd = len(shape)
    return pl.BlockSpec(shape, lambda *_: (0,) * nd)


def _rope_kernel(invd_ref, invrt_ref, invr_ref, sgn_ref,
                 cosd_ref, sind_ref, cosrt_ref, sinrt_ref, cosr_ref, sinr_ref):
    ts = cosd_ref.shape[1]
    base = pl.program_id(0) * ts
    pos_d = (base + lax.broadcasted_iota(jnp.int32, (8, ts), 1)).astype(F32)
    ang_d = pos_d * invd_ref[...]
    cosd_ref[...] = jnp.cos(ang_d)
    sind_ref[...] = jnp.sin(ang_d)
    pos_r = (base + lax.broadcasted_iota(jnp.int32, (32, ts), 1)).astype(F32)
    ang_r = pos_r * invrt_ref[...]
    cosrt_ref[...] = jnp.cos(ang_r)
    sinrt_ref[...] = jnp.sin(ang_r)
    pos_t = (base + lax.broadcasted_iota(jnp.int32, (ts, LANES), 0)).astype(F32)
    ang_t = pos_t * invr_ref[0:1, :]
    cosr_ref[...] = jnp.cos(ang_t)
    sinr_ref[...] = jnp.sin(ang_t) * sgn_ref[0:1, :]


def _rope_tables(s_max):
    ts = min(TABLE_TILE, s_max)
    inv_d = 1.0 / (ROPE_THETA ** (jnp.arange(0, ROT_DIMS, 2, dtype=F32) / ROT_DIMS))
    inv_r = 1.0 / (RET_THETA ** (jnp.arange(0, RET_HEAD_DIM, 2, dtype=F32) / RET_HEAD_DIM))
    invd_b = jnp.broadcast_to(inv_d[:, None], (8, ts))
    invrt_b = jnp.broadcast_to(inv_r[:, None], (32, ts))
    invr_lane = jnp.broadcast_to(jnp.tile(inv_r, LANES // 32)[None, :], (8, LANES))
    sgn = jnp.where((jnp.arange(LANES) % RET_HEAD_DIM) < RET_HEAD_DIM // 2, -1.0, 1.0).astype(F32)
    sgn_b = jnp.broadcast_to(sgn[None, :], (8, LANES))
    return pl.pallas_call(
        _rope_kernel,
        grid=(s_max // ts,),
        in_specs=[_const_spec((8, ts)), _const_spec((32, ts)), _const_spec((8, LANES)),
                  _const_spec((8, LANES))],
        out_specs=[pl.BlockSpec((8, ts), lambda i: (0, i)), pl.BlockSpec((8, ts), lambda i: (0, i)),
                   pl.BlockSpec((32, ts), lambda i: (0, i)), pl.BlockSpec((32, ts), lambda i: (0, i)),
                   pl.BlockSpec((ts, LANES), lambda i: (i, 0)), pl.BlockSpec((ts, LANES), lambda i: (i, 0))],
        out_shape=[jax.ShapeDtypeStruct((8, s_max), F32), jax.ShapeDtypeStruct((8, s_max), F32),
                   jax.ShapeDtypeStruct((32, s_max), F32), jax.ShapeDtypeStruct((32, s_max), F32),
                   jax.ShapeDtypeStruct((s_max, LANES), F32), jax.ShapeDtypeStruct((s_max, LANES), F32)],
        compiler_params=_cparams(1),
        name="rope_tables",
    )(invd_b, invrt_b, invr_lane, sgn_b)


def _inproj_kernel(x_ref, ln_ref, wf_ref, wt_ref, qw_ref, kw_ref,
                   cd_ref, sd_ref, crt_ref, srt_ref, cr_ref, sr_ref,
                   qt_ref, k_ref, v_ref, rq_ref, rkt_ref, rv_ref, rg_ref):
    tm = x_ref.shape[0]
    x = x_ref[...]
    ms = jnp.mean(x * x, axis=-1, keepdims=True)
    hn = (x * lax.rsqrt(ms + EPS) * ln_ref[...]).astype(BF16)
    feat = lax.dot_general(wf_ref[...], hn, (((1,), (1,)), ((), ())), preferred_element_type=F32)
    tok = jnp.dot(hn, wt_ref[...], preferred_element_type=F32)

    cd = cd_ref[...]
    sd = sd_ref[...]

    def norm_rot(z, w):
        z3 = z.reshape(2 * DIFF_HEADS, QK_DIM, tm)
        gms = jnp.mean(z3 * z3, axis=1, keepdims=True)
        y = z3 * lax.rsqrt(gms + EPS) * w[None]
        y1 = y[:, 0:8]
        y2 = y[:, 8:16]
        out = jnp.concatenate([y1 * cd - y2 * sd, y1 * sd + y2 * cd, y[:, 16:]], axis=1)
        return out.reshape(DIFF_WIDTH, tm)

    qt_ref[...] = norm_rot(feat[0:512], qw_ref[...]).astype(BF16)
    k_ref[...] = norm_rot(feat[512:1024], kw_ref[...]).T.astype(BF16)

    v3 = feat[1024:1536].reshape(DIFF_HEADS, DV_DIM, tm).astype(BF16)
    v_ref[...] = v3

    half = RET_HEAD_DIM // 2
    rk3 = feat[1536:2048].reshape(RET_HEADS, RET_HEAD_DIM, tm)
    crt = crt_ref[...]
    srt = srt_ref[...]
    r1 = rk3[:, 0:half]
    r2 = rk3[:, half:]
    rk_rot = jnp.concatenate([r1 * crt - r2 * srt, r1 * srt + r2 * crt], axis=1)
    rkt_ref[...] = (rk_rot * (RET_HEAD_DIM ** -0.5)).reshape(RET_WIDTH, tm).astype(BF16)

    rq = tok[:, 0:512]
    lane = lax.broadcasted_iota(jnp.int32, rq.shape, 1)
    partner = jnp.where((lane % RET_HEAD_DIM) < half,
                        pltpu.roll(rq, RET_WIDTH - half, 1), pltpu.roll(rq, half, 1))
    cr = jnp.concatenate([cr_ref[...]] * (RET_WIDTH // LANES), axis=1)
    sr = jnp.concatenate([sr_ref[...]] * (RET_WIDTH // LANES), axis=1)
    rq_ref[...] = (rq * cr + partner * sr).astype(BF16)
    rv_ref[...] = tok[:, 512:1024].astype(BF16)
    rg_ref[...] = tok[:, 1024:1536]


def _inproj(x, ln1, wf, wt, qw_b, kw_b, tabs, layer, batch, seq):
    t = batch * seq
    tm = min(TOKEN_TILE, seq)
    nst = seq // tm
    cosd, sind, cosrt, sinrt, cosr, sinr = tabs
    lay3 = lambda i: (layer, 0, 0)
    in_specs = [
        pl.BlockSpec((tm, D_MODEL), lambda i: (i, 0)),
        pl.BlockSpec((None, 1, D_MODEL), lay3),
        pl.BlockSpec((None, 2048, D_MODEL), lay3),
        pl.BlockSpec((None, D_MODEL, 1536), lay3),
        pl.BlockSpec((None, QK_DIM, tm), lay3),
        pl.BlockSpec((None, QK_DIM, tm), lay3),
        pl.BlockSpec((8, tm), lambda i: (0, i % nst)),
        pl.BlockSpec((8, tm), lambda i: (0, i % nst)),
        pl.BlockSpec((32, tm), lambda i: (0, i % nst)),
        pl.BlockSpec((32, tm), lambda i: (0, i % nst)),
        pl.BlockSpec((tm, LANES), lambda i: (i % nst, 0)),
        pl.BlockSpec((tm, LANES), lambda i: (i % nst, 0)),
    ]
    feat_spec = pl.BlockSpec((None, 512, tm), lambda i: (i // nst, 0, i % nst))
    tok_spec = pl.BlockSpec((tm, 512), lambda i: (i, 0))
    out_specs = [
        feat_spec,
        tok_spec,
        pl.BlockSpec((None, DIFF_HEADS, V_ROWS, tm), lambda i: (i // nst, 0, 0, i % nst)),
        tok_spec,
        feat_spec,
        tok_spec,
        tok_spec,
    ]
    out_shape = [
        jax.ShapeDtypeStruct((batch, 512, seq), BF16),
        jax.ShapeDtypeStruct((t, 512), BF16),
        jax.ShapeDtypeStruct((batch, DIFF_HEADS, V_ROWS, seq), BF16),
        jax.ShapeDtypeStruct((t, 512), BF16),
        jax.ShapeDtypeStruct((batch, 512, seq), BF16),
        jax.ShapeDtypeStruct((t, 512), BF16),
        jax.ShapeDtypeStruct((t, 512), F32),
    ]
    return pl.pallas_call(
        _inproj_kernel, grid=(t // tm,), in_specs=in_specs, out_specs=out_specs, out_shape=out_shape,
        compiler_params=_cparams(1), name="inproj",
    )(x, ln1, wf, wt, qw_b, kw_b, cosd, sind, cosrt, sinrt, cosr, sinr)


def _attn_kernel(lp_ref, qw_ref, kw_ref, qt_ref, k_ref, v_ref, sw_ref, o_ref, acc1, acc2, m1, m2, l1, l2,
                 *, lam_init, tk, tk_fast):
    seq = k_ref.shape[0]
    qt = qt_ref[...]
    row = lax.broadcasted_iota(jnp.int32, qt.shape, 0)
    zero = jnp.zeros_like(qt)
    qa = jnp.where(row < QK_DIM, qt, zero)
    qb = jnp.where(row >= QK_DIM, qt, zero)
    comps = ((qa, acc1, m1, l1), (qb, acc2, m2, l2))
    lp = lp_ref[...]
    lam = (jnp.exp(jnp.sum(lp[0:1] * lp[1:2], axis=1, keepdims=True))
           - jnp.exp(jnp.sum(lp[2:3] * lp[3:4], axis=1, keepdims=True)) + lam_init)

    def finish():
        out = acc1[...] / l1[...] - lam * (acc2[...] / l2[...])
        oms = jnp.mean(out * out, axis=0, keepdims=True)
        y = out * lax.rsqrt(oms + EPS) * sw_ref[...]
        o_ref[...] = y.T.astype(BF16)

    bound = (jnp.max(jnp.abs(qw_ref[...]), axis=1, keepdims=True)
             * jnp.max(jnp.abs(kw_ref[...]), axis=1, keepdims=True) * (QK_DIM * BOUND_SLACK))
    acc1[...] = jnp.zeros_like(acc1)
    acc2[...] = jnp.zeros_like(acc2)
    l1[...] = jnp.zeros_like(l1)
    l2[...] = jnp.zeros_like(l2)

    def bound_body(j, carry):
        off = pl.multiple_of(j * tk_fast, tk_fast)
        kb = k_ref[pl.ds(off, tk_fast), :]
        vb = v_ref[:, pl.ds(off, tk_fast)]
        for qx, acc, _, l in comps:
            s = jnp.dot(kb, qx, preferred_element_type=F32)
            e = jnp.exp2(s - bound)
            l[...] += jnp.sum(e, axis=0, keepdims=True)
            acc[...] += jnp.dot(vb, e.astype(BF16), preferred_element_type=F32)
        return carry

    lax.fori_loop(0, seq // tk_fast, bound_body, 0)
    l_min = jnp.minimum(jnp.min(l1[...]), jnp.min(l2[...]))
    finish()

    @pl.when(l_min < UNDERFLOW_GUARD)
    def _running_max_pass():
        acc1[...] = jnp.zeros_like(acc1)
        acc2[...] = jnp.zeros_like(acc2)
        m1[...] = jnp.full_like(m1, NEG_BIG)
        m2[...] = jnp.full_like(m2, NEG_BIG)
        l1[...] = jnp.zeros_like(l1)
        l2[...] = jnp.zeros_like(l2)

        def body(j, carry):
            off = pl.multiple_of(j * tk, tk)
            kb = k_ref[pl.ds(off, tk), :]
            vb = v_ref[:, pl.ds(off, tk)]
            for qx, acc, m, l in comps:
                s = jnp.dot(kb, qx, preferred_element_type=F32)
                m_old = m[...]
                m_new = jnp.maximum(m_old, jnp.max(s, axis=0, keepdims=True))
                e = jnp.exp2(s - m_new)
                alpha = jnp.exp2(m_old - m_new)
                l[...] = l[...] * alpha + jnp.sum(e, axis=0, keepdims=True)
                acc[...] = acc[...] * alpha + jnp.dot(vb, e.astype(BF16), preferred_element_type=F32)
                m[...] = m_new
            return carry

        lax.fori_loop(0, seq // tk, body, 0)
        finish()


def _diff_attention(lp, qw, kw, qt, k, v, sw_b, layer, lam_init, batch, seq):
    tq = min(Q_TILE, seq)
    tk = min(KV_TILE, seq)
    tk_fast = min(KV_TILE_FAST, seq)
    k3 = k.reshape(batch, seq, DIFF_WIDTH)
    kern = functools.partial(_attn_kernel, lam_init=lam_init, tk=tk, tk_fast=tk_fast)
    lay3 = lambda b, h, i: (layer, 0, 0)
    return pl.pallas_call(
        kern,
        grid=(batch, DIFF_HEADS, seq // tq),
        in_specs=[
            pl.BlockSpec((None, 4, QK_DIM), lay3),
            pl.BlockSpec((None, 1, QK_DIM), lay3),
            pl.BlockSpec((None, 1, QK_DIM), lay3),
            pl.BlockSpec((None, DV_DIM, tq), lambda b, h, i: (b, h, i)),
            pl.BlockSpec((None, seq, DV_DIM), lambda b, h, i: (b, 0, h)),
            pl.BlockSpec((None, None, V_ROWS, seq), lambda b, h, i: (b, h, 0, 0)),
            pl.BlockSpec((None, DV_DIM, tq), lay3),
        ],
        out_specs=pl.BlockSpec((None, tq, DV_DIM), lambda b, h, i: (b, i, h)),
        out_shape=jax.ShapeDtypeStruct((batch, seq, DIFF_WIDTH), BF16),
        scratch_shapes=[pltpu.VMEM((V_ROWS, tq), F32), pltpu.VMEM((V_ROWS, tq), F32),
                        pltpu.VMEM((1, tq), F32), pltpu.VMEM((1, tq), F32),
                        pltpu.VMEM((1, tq), F32), pltpu.VMEM((1, tq), F32)],
        compiler_params=_cparams(3), name="diff_attention",
    )(lp, qw, kw, qt, k3, v, sw_b)


def _ret_kernel(lgl_ref, lgr_ref, lgs_ref, gnw_ref, q_ref, kt_ref, v_ref, g_ref, o_ref,
                qdf, qdb, kdf, kdb, cdf, cdb, dcomb, gfull, s_st, t_st, t_all, o_buf, *, group):
    b = pl.program_id(0)
    ph = pl.program_id(1)
    j = pl.program_id(2)
    n_groups = pl.num_programs(2)
    c = CHUNK

    @pl.when((b == 0) & (ph == 0) & (j == 0))
    def _tables():
        lgl = jax.nn.log_sigmoid(lgl_ref[...])
        lgr = jax.nn.log_sigmoid(lgr_ref[...])
        lgs = jax.nn.log_sigmoid(lgs_ref[...])
        pos_row = lax.broadcasted_iota(jnp.int32, (c, RET_WIDTH), 0).astype(F32)
        qdf[...] = jnp.exp((pos_row + 1.0) * lgl[0, 0:1, :])
        qdb[...] = jnp.exp((c - pos_row) * lgl[1, 0:1, :])
        pos_lane = lax.broadcasted_iota(jnp.int32, (RET_WIDTH, c), 1).astype(F32)
        kdf[...] = jnp.exp((c - 1.0 - pos_lane) * lgr[0])
        kdb[...] = jnp.exp(pos_lane * lgr[1])
        cdf[...] = jnp.exp(float(c) * lgr[0])
        cdb[...] = jnp.exp(float(c) * lgr[1])
        qi = (lax.broadcasted_iota(jnp.int32, (RET_HEADS * c, c), 0) % c).astype(F32)
        ki = lax.broadcasted_iota(jnp.int32, (RET_HEADS * c, c), 1).astype(F32)
        rel = qi - ki
        dcomb[...] = jnp.where(rel >= 0, jnp.exp(rel * lgs[0]), jnp.exp(-rel * lgs[1]))

        frow = lax.broadcasted_iota(jnp.int32, (RET_WIDTH, RET_WIDTH), 0) // RET_HEAD_DIM
        fcol = lax.broadcasted_iota(jnp.int32, (RET_WIDTH, RET_WIDTH), 1) // RET_HEAD_DIM
        gfull[...] = jnp.where(frow == fcol, 1.0 / RET_HEAD_DIM, 0.0).astype(BF16)

    lane = lax.broadcasted_iota(jnp.int32, (c, c), 1)
    row = lax.broadcasted_iota(jnp.int32, (c, c), 0)
    lo = lane < RET_HEAD_DIM
    same_head = (row < RET_HEAD_DIM) == lo

    @pl.when((ph == 0) & (j == 0))
    def _init():
        t_st[...] = jnp.zeros_like(t_st)
        s_st[...] = jnp.zeros_like(s_st)

    @pl.when(ph == 0)
    def _backward_states():
        first = (n_groups - 1 - j) * group
        for p in range(RET_PAIRS):
            sl = slice(p * c, (p + 1) * c)
            t = t_st[p]
            for g in reversed(range(group)):
                gs = slice(g * c, (g + 1) * c)
                t_all[first + g, p] = t.astype(BF16)
                ktd = (kt_ref[sl, gs].astype(F32) * kdb[sl, :]).astype(BF16)
                kv = jnp.dot(ktd, v_ref[gs, sl], preferred_element_type=F32)
                t = t * cdb[sl, :] + jnp.where(same_head, kv, 0.0)
            t_st[p] = t

    @pl.when(ph == 1)
    def _forward():
        first = j * group
        combos = [(p, g) for p in range(RET_PAIRS) for g in range(group)]
        zq = jnp.zeros((c, c), BF16)

        def blk(p, g):
            return slice(g * c, (g + 1) * c), slice(p * c, (p + 1) * c)

        kvs, sds = {}, {}
        for p, g in combos:
            gs, sl = blk(p, g)
            ktp = kt_ref[sl, gs]
            ktd = (ktp.astype(F32) * kdf[sl, :]).astype(BF16)
            kvs[p, g] = jnp.dot(ktd, v_ref[gs, sl], preferred_element_type=F32)
            qp = q_ref[gs, sl]
            q2 = jnp.concatenate([jnp.where(lo, qp, zq), jnp.where(lo, zq, qp)], axis=0)
            s = jnp.dot(q2, ktp, preferred_element_type=F32)
            sds[p, g] = (s * dcomb[2 * p * c:2 * (p + 1) * c, :]).astype(BF16)
        states = {}
        for p in range(RET_PAIRS):
            sl = slice(p * c, (p + 1) * c)
            s_state = s_st[p]
            for g in range(group):
                states[p, g] = s_state.astype(BF16)
                s_state = s_state * cdf[sl, :] + jnp.where(same_head, kvs[p, g], 0.0)
            s_st[p] = s_state
        for p, g in combos:
            gs, sl = blk(p, g)
            qp = q_ref[gs, sl]
            vp = v_ref[gs, sl]
            qf = qp.astype(F32)
            sd = sds[p, g]
            lhs = jnp.concatenate([sd[0:c], sd[c:2 * c], (qf * qdf[:, sl]).astype(BF16),
                                   (qf * qdb[:, sl]).astype(BF16)], axis=1)
            rhs = jnp.concatenate([jnp.where(lo, vp, zq), jnp.where(lo, zq, vp),
                                   states[p, g], t_all[first + g, p]], axis=0)
            o_buf[gs, sl] = jnp.dot(lhs, rhs, preferred_element_type=F32)
        o = o_buf[...]
        gms = jnp.dot((o * o).astype(BF16), gfull[...], preferred_element_type=F32)
        r = o * lax.rsqrt(gms + EPS) * gnw_ref[0:1, :]
        gate = g_ref[...]
        o_ref[...] = (gate * jax.nn.sigmoid(gate) * r).astype(BF16)


def _retention(lg_lane, lg_row, lg_sc, gnw_b, rq, rkt, rv, rg, layer, batch, seq):
    c = CHUNK
    n = seq // c
    group = min(RET_GROUP, n)
    ng = n // group
    tg = group * c

    def tok_map(b, ph, j):
        return (b * ng + jnp.where(ph == 0, ng - 1 - j, j), 0)

    def feat_map(b, ph, j):
        return (b, 0, jnp.where(ph == 0, ng - 1 - j, j))

    def fwd_only_map(b, ph, j):
        return (b * ng + jnp.where(ph == 0, 0, j), 0)

    return pl.pallas_call(
        functools.partial(_ret_kernel, group=group),
        grid=(batch, 2, ng),
        in_specs=[
            pl.BlockSpec((None, 2, 8, RET_WIDTH), lambda b, ph, j: (layer, 0, 0, 0)),
            pl.BlockSpec((None, 2, RET_WIDTH, c), lambda b, ph, j: (layer, 0, 0, 0)),
            pl.BlockSpec((None, 2, RET_HEADS * c, c), lambda b, ph, j: (layer, 0, 0, 0)),
            pl.BlockSpec((None, 8, RET_WIDTH), lambda b, ph, j: (layer, 0, 0)),
            pl.BlockSpec((tg, RET_WIDTH), fwd_only_map),
            pl.BlockSpec((None, RET_WIDTH, tg), feat_map),
            pl.BlockSpec((tg, RET_WIDTH), tok_map),
            pl.BlockSpec((tg, RET_WIDTH), fwd_only_map),
        ],
        out_specs=pl.BlockSpec((tg, RET_WIDTH), fwd_only_map),
        out_shape=jax.ShapeDtypeStruct((batch * seq, RET_WIDTH), BF16),
        scratch_shapes=[
            pltpu.VMEM((c, RET_WIDTH), F32), pltpu.VMEM((c, RET_WIDTH), F32),
            pltpu.VMEM((RET_WIDTH, c), F32), pltpu.VMEM((RET_WIDTH, c), F32),
            pltpu.VMEM((RET_WIDTH, c), F32), pltpu.VMEM((RET_WIDTH, c), F32),
            pltpu.VMEM((RET_HEADS * c, c), F32),
            pltpu.VMEM((RET_WIDTH, RET_WIDTH), BF16),
            pltpu.VMEM((RET_PAIRS, c, c), F32), pltpu.VMEM((RET_PAIRS, c, c), F32),
            pltpu.VMEM((n, RET_PAIRS, c, c), BF16),
            pltpu.VMEM((tg, RET_WIDTH), F32),
        ],
        compiler_params=_cparams(3), name="retention",
    )(lg_lane, lg_row, lg_sc, gnw_b, rq, rkt, rv, rg)


def _post_kernel(x_ref, a_ref, r_ref, p_ref, wo_ref, ln2_ref, w1_ref, w2_ref, wg_ref, wp_ref, o_ref):
    x = x_ref[...]
    x = x + jnp.dot(a_ref[...], wo_ref[0:DIFF_WIDTH, :], preferred_element_type=F32)
    x = x + jnp.dot(r_ref[...], wo_ref[DIFF_WIDTH:, :], preferred_element_type=F32)
    ms = jnp.mean(x * x, axis=-1, keepdims=True)
    h2 = (x * lax.rsqrt(ms + EPS) * ln2_ref[...]).astype(BF16)
    hid = jnp.dot(h2, w1_ref[...], preferred_element_type=F32)
    act = jnp.square(jnp.maximum(hid, 0.0)).astype(BF16)
    x = x + jnp.dot(act, w2_ref[...], preferred_element_type=F32)
    gate = jax.nn.sigmoid(jnp.dot(x.astype(BF16), wg_ref[...], preferred_element_type=F32))
    pe = jnp.dot(p_ref[...].astype(BF16), wp_ref[...], preferred_element_type=F32)
    o_ref[...] = x + gate * pe


def _post(x, a, r, p, wo, ln2, w1, w2, wg, wp, layer, batch, seq):
    t = batch * seq
    tm = min(TOKEN_TILE, seq)
    lay3 = lambda i: (layer, 0, 0)

    def wspec(shape):
        return pl.BlockSpec((None,) + shape, lay3, pipeline_mode=pl.Buffered(1))

    return pl.pallas_call(
        _post_kernel,
        grid=(t // tm,),
        in_specs=[
            pl.BlockSpec((tm, D_MODEL), lambda i: (i, 0)),
            pl.BlockSpec((tm, DIFF_WIDTH), lambda i: (i, 0)),
            pl.BlockSpec((tm, RET_WIDTH), lambda i: (i, 0)),
            pl.BlockSpec((None, tm, PLE_DIM), lambda i: (layer, i, 0)),
            wspec((D_MODEL, D_MODEL)),
            pl.BlockSpec((None, 1, D_MODEL), lay3),
            wspec((D_MODEL, D_FF)),
            wspec((D_FF, D_MODEL)),
            wspec((D_MODEL, D_MODEL)),
            wspec((PLE_DIM, D_MODEL)),
        ],
        out_specs=pl.BlockSpec((tm, D_MODEL), lambda i: (i, 0)),
        out_shape=jax.ShapeDtypeStruct((t, D_MODEL), F32),
        compiler_params=_cparams(1), name="post",
    )(x, a, r, p, wo, ln2, w1, w2, wg, wp)


def _prepare_params(ln1_w, w_in, diff_q_norm, diff_k_norm, diff_subln, ret_decay_logit, ret_gn,
                    w_out, ln2_w, w_mlp1, w_mlp2, w_ple_gate, w_ple_proj):
    depth = w_in.shape[0]
    tm = TOKEN_TILE
    dq, dk, dv = w_in[:, :, 0:512], w_in[:, :, 512:1024], w_in[:, :, 1024:1536]
    rq, rk = w_in[:, :, 1536:2048], w_in[:, :, 2048:2560]
    rv, rg = w_in[:, :, 2560:3072], w_in[:, :, 3072:3584]
    wf = jnp.swapaxes(jnp.concatenate([dq, dk, dv, rk], axis=2), 1, 2).astype(BF16)
    wt = jnp.concatenate([rq, rv, rg], axis=2).astype(BF16)
    q_scale = (QK_DIM ** -0.5) * LOG2E
    qw_b = jnp.broadcast_to((diff_q_norm.astype(F32) * q_scale)[:, :, None], (depth, QK_DIM, tm))
    kw_b = jnp.broadcast_to(diff_k_norm.astype(F32)[:, :, None], (depth, QK_DIM, tm))
    lam_init = [0.8 - 0.6 * math.exp(-0.3 * i) for i in range(depth)]
    sub_scale = jnp.asarray([1.0 - li for li in lam_init], F32)[:, None]
    sw_b = jnp.broadcast_to((diff_subln.astype(F32) * sub_scale)[:, :, None], (depth, DV_DIM, Q_TILE))
    lgt = ret_decay_logit.astype(F32)
    lg_lane = jnp.broadcast_to(jnp.repeat(lgt, RET_HEAD_DIM, axis=2)[:, :, None, :], (depth, 2, 8, RET_WIDTH))
    lg_row = jnp.broadcast_to(jnp.repeat(lgt, RET_HEAD_DIM, axis=2)[:, :, :, None], (depth, 2, RET_WIDTH, CHUNK))
    lg_sc = jnp.broadcast_to(jnp.repeat(lgt, CHUNK, axis=2)[:, :, :, None], (depth, 2, RET_HEADS * CHUNK, CHUNK))
    gnw_b = jnp.broadcast_to(jnp.tile(ret_gn.astype(F32), (1, RET_HEADS))[:, None, :], (depth, 8, RET_WIDTH))
    return dict(
        ln1=ln1_w.astype(F32)[:, None, :], wf=wf, wt=wt, qw_b=qw_b, kw_b=kw_b, sw_b=sw_b, lam_init=lam_init,
        qw=(diff_q_norm.astype(F32) * q_scale)[:, None, :], kw=diff_k_norm.astype(F32)[:, None, :],
        lg_lane=lg_lane, lg_row=lg_row, lg_sc=lg_sc, gnw_b=gnw_b,
        wo=w_out.astype(BF16), ln2=ln2_w.astype(F32)[:, None, :], w1=w_mlp1.astype(BF16),
        w2=w_mlp2.astype(BF16), wg=w_ple_gate.astype(BF16), wp=w_ple_proj.astype(BF16))


def _run_trunk(x, p, prm, tabs, diff_lambda):
    batch, seq, _ = x.shape
    depth = p.shape[0]
    xf = x.reshape(batch * seq, D_MODEL)
    pf = p.reshape(depth, batch * seq, PLE_DIM)
    lp = diff_lambda.astype(F32)
    for i in range(depth):
        qt, k, v, rq, rkt, rv, rg = _inproj(xf, prm["ln1"], prm["wf"], prm["wt"], prm["qw_b"], prm["kw_b"],
                                            tabs, i, batch, seq)
        a = _diff_attention(lp, prm["qw"], prm["kw"], qt, k, v, prm["sw_b"], i, prm["lam_init"][i],
                            batch, seq)
        r = _retention(prm["lg_lane"], prm["lg_row"], prm["lg_sc"], prm["gnw_b"], rq, rkt, rv, rg,
                       i, batch, seq)
        xf = _post(xf, a.reshape(batch * seq, DIFF_WIDTH), r, pf, prm["wo"], prm["ln2"], prm["w1"],
                   prm["w2"], prm["wg"], prm["wp"], i, batch, seq)
    return xf.reshape(batch, seq, D_MODEL)


def kernel(x_prompt, x_sample, p_prompt, p_sample, ln1_w, w_in, diff_q_norm, diff_k_norm, diff_lambda,
           diff_subln, ret_decay_logit, ret_gn, w_out, ln2_w, w_mlp1, w_mlp2, w_ple_gate, w_ple_proj):
    prm = _prepare_params(ln1_w, w_in, diff_q_norm, diff_k_norm, diff_subln, ret_decay_logit, ret_gn,
                          w_out, ln2_w, w_mlp1, w_mlp2, w_ple_gate, w_ple_proj)
    tabs = _rope_tables(max(x_prompt.shape[1], x_sample.shape[1]))
    y_prompt = _run_trunk(x_prompt, p_prompt, prm, tabs, diff_lambda)
    y_sample = _run_trunk(x_sample, p_sample, prm, tabs, diff_lambda)
    return (y_prompt, y_sample)
```

```python
import functools
import math

import jax
import jax.numpy as jnp
from jax import lax
from jax.experimental import pallas as pl
from jax.experimental.pallas import tpu as pltpu

F32 = jnp.float32
BF16 = jnp.bfloat16

D_MODEL = 1024
DIFF_WIDTH = 512
RET_WIDTH = 512
QK_DIM = 64
DV_DIM = 128
DIFF_HEADS = 4
RET_HEAD_DIM = 64
RET_HEADS = 8
RET_PAIRS = RET_HEADS // 2
ROT_DIMS = 16
ROPE_THETA = 500000.0
RET_THETA = 10000.0
D_FF = 4096
PLE_DIM = 256
CHUNK = 128
EPS = 1e-6
LOG2E = 1.4426950408889634
NEG_BIG = -1e30

VMEM_LIMIT_BYTES = 56 * 1024 * 1024
LANES = 128
V_ROWS = DV_DIM

TOKEN_TILE = 1024
POST_SUB = 512
INPROJ_TILE = 1024
INPROJ_SUB = 256
Q_TILE = 1024
KV_TILE = 256
KV_TILE_FAST = 4096
BOUND_SLACK = 1.02
UNDERFLOW_GUARD = 2.0 ** -64
RET_GROUP = 8
TABLE_TILE = 1024


def _cparams(n_grid):
    return pltpu.CompilerParams(dimension_semantics=("arbitrary",) * n_grid,
                                vmem_limit_bytes=VMEM_LIMIT_BYTES)


def _const_spec(shape):
    nd = len(shape)
    return pl.BlockSpec(shape, lambda *_: (0,) * nd)


def _rope_kernel(invd_ref, invrt_ref, invr_ref, sgn_ref,
                 cosd_ref, sind_ref, cosrt_ref, sinrt_ref, cosr_ref, sinr_ref):
    ts = cosd_ref.shape[1]
    base = pl.program_id(0) * ts
    pos_d = (base + lax.broadcasted_iota(jnp.int32, (8, ts), 1)).astype(F32)
    ang_d = pos_d * invd_ref[...]
    cosd_ref[...] = jnp.cos(ang_d)
    sind_ref[...] = jnp.sin(ang_d)
    pos_r = (base + lax.broadcasted_iota(jnp.int32, (32, ts), 1)).astype(F32)
    ang_r = pos_r * invrt_ref[...]
    cosrt_ref[...] = jnp.cos(ang_r)
    sinrt_ref[...] = jnp.sin(ang_r)
    pos_t = (base + lax.broadcasted_iota(jnp.int32, (ts, LANES), 0)).astype(F32)
    ang_t = pos_t * invr_ref[0:1, :]
    cosr_ref[...] = jnp.cos(ang_t)
    sinr_ref[...] = jnp.sin(ang_t) * sgn_ref[0:1, :]


def _rope_tables(s_max):
    ts = min(TABLE_TILE, s_max)
    inv_d = 1.0 / (ROPE_THETA ** (jnp.arange(0, ROT_DIMS, 2, dtype=F32) / ROT_DIMS))
    inv_r = 1.0 / (RET_THETA ** (jnp.arange(0, RET_HEAD_DIM, 2, dtype=F32) / RET_HEAD_DIM))
    invd_b = jnp.broadcast_to(inv_d[:, None], (8, ts))
    invrt_b = jnp.broadcast_to(inv_r[:, None], (32, ts))
    invr_lane = jnp.broadcast_to(jnp.tile(inv_r, LANES // 32)[None, :], (8, LANES))
    sgn = jnp.where((jnp.arange(LANES) % RET_HEAD_DIM) < RET_HEAD_DIM // 2, -1.0, 1.0).astype(F32)
    sgn_b = jnp.broadcast_to(sgn[None, :], (8, LANES))
    return pl.pallas_call(
        _rope_kernel,
        grid=(s_max // ts,),
        in_specs=[_const_spec((8, ts)), _const_spec((32, ts)), _const_spec((8, LANES)),
                  _const_spec((8, LANES))],
        out_specs=[pl.BlockSpec((8, ts), lambda i: (0, i)), pl.BlockSpec((8, ts), lambda i: (0, i)),
                   pl.BlockSpec((32, ts), lambda i: (0, i)), pl.BlockSpec((32, ts), lambda i: (0, i)),
                   pl.BlockSpec((ts, LANES), lambda i: (i, 0)), pl.BlockSpec((ts, LANES), lambda i: (i, 0))],
        out_shape=[jax.ShapeDtypeStruct((8, s_max), F32), jax.ShapeDtypeStruct((8, s_max), F32),
                   jax.ShapeDtypeStruct((32, s_max), F32), jax.ShapeDtypeStruct((32, s_max), F32),
                   jax.ShapeDtypeStruct((s_max, LANES), F32), jax.ShapeDtypeStruct((s_max, LANES), F32)],
        compiler_params=_cparams(1),
        name="rope_tables",
    )(invd_b, invrt_b, invr_lane, sgn_b)


def _inproj_kernel(x_ref, ln_ref, wf_ref, wt_ref, qw_ref, kw_ref,
                   cd_ref, sd_ref, crt_ref, srt_ref, cr_ref, sr_ref,
                   qt_ref, k_ref, v_ref, rq_ref, rkt_ref, rv_ref, rg_ref):
    tm = min(INPROJ_SUB, x_ref.shape[0])
    half = RET_HEAD_DIM // 2

    def norm_rot(z, w, cd, sd):
        z3 = z.reshape(2 * DIFF_HEADS, QK_DIM, tm)
        gms = jnp.mean(z3 * z3, axis=1, keepdims=True)
        y = z3 * lax.rsqrt(gms + EPS) * w[None]
        y1 = y[:, 0:8]
        y2 = y[:, 8:16]
        out = jnp.concatenate([y1 * cd - y2 * sd, y1 * sd + y2 * cd, y[:, 16:]], axis=1)
        return out.reshape(DIFF_WIDTH, tm)

    for part in range(x_ref.shape[0] // tm):
        ts = slice(part * tm, (part + 1) * tm)
        x = x_ref[ts, :]
        ms = jnp.mean(x * x, axis=-1, keepdims=True)
        hn = (x * lax.rsqrt(ms + EPS) * ln_ref[...]).astype(BF16)

        def feat_dot(lo, hi):
            return lax.dot_general(wf_ref[lo:hi, :], hn, (((1,), (1,)), ((), ())),
                                   preferred_element_type=F32)

        def tok_dot(lo, hi):
            return jnp.dot(hn, wt_ref[:, lo:hi], preferred_element_type=F32)

        cd = cd_ref[:, ts]
        sd = sd_ref[:, ts]
        k_ref[ts, :] = norm_rot(feat_dot(512, 1024), kw_ref[...], cd, sd).T.astype(BF16)
        qt_ref[:, ts] = norm_rot(feat_dot(0, 512), qw_ref[...], cd, sd).astype(BF16)

        rq = tok_dot(0, 512)
        lane = lax.broadcasted_iota(jnp.int32, rq.shape, 1)
        partner = jnp.where((lane % RET_HEAD_DIM) < half,
                            pltpu.roll(rq, RET_WIDTH - half, 1), pltpu.roll(rq, half, 1))
        cr = jnp.concatenate([cr_ref[ts, :]] * (RET_WIDTH // LANES), axis=1)
        sr = jnp.concatenate([sr_ref[ts, :]] * (RET_WIDTH // LANES), axis=1)
        rq_ref[ts, :] = (rq * cr + partner * sr).astype(BF16)

        rk3 = feat_dot(1536, 2048).reshape(RET_HEADS, RET_HEAD_DIM, tm)
        crt = crt_ref[:, ts]
        srt = srt_ref[:, ts]
        r1 = rk3[:, 0:half]
        r2 = rk3[:, half:]
        rk_rot = jnp.concatenate([r1 * crt - r2 * srt, r1 * srt + r2 * crt], axis=1)
        rkt_ref[:, ts] = (rk_rot * (RET_HEAD_DIM ** -0.5)).reshape(RET_WIDTH, tm).astype(BF16)

        v_ref[:, :, ts] = feat_dot(1024, 1536).reshape(DIFF_HEADS, DV_DIM, tm).astype(BF16)
        rv_ref[ts, :] = tok_dot(512, 1024).astype(BF16)
        rg_ref[ts, :] = tok_dot(1024, 1536)


def _inproj(x, ln1, wf, wt, qw_b, kw_b, tabs, layer, batch, seq):
    t = batch * seq
    tm = min(INPROJ_TILE, seq)
    nst = seq // tm
    cosd, sind, cosrt, sinrt, cosr, sinr = tabs
    lay3 = lambda i: (layer, 0, 0)
    in_specs = [
        pl.BlockSpec((tm, D_MODEL), lambda i: (i, 0)),
        pl.BlockSpec((None, 1, D_MODEL), lay3),
        pl.BlockSpec((None, 2048, D_MODEL), lay3),
        pl.BlockSpec((None, D_MODEL, 1536), lay3),
        pl.BlockSpec((None, QK_DIM, min(INPROJ_SUB, tm)), lay3),
        pl.BlockSpec((None, QK_DIM, min(INPROJ_SUB, tm)), lay3),
        pl.BlockSpec((8, tm), lambda i: (0, i % nst)),
        pl.BlockSpec((8, tm), lambda i: (0, i % nst)),
        pl.BlockSpec((32, tm), lambda i: (0, i % nst)),
        pl.BlockSpec((32, tm), lambda i: (0, i % nst)),
        pl.BlockSpec((tm, LANES), lambda i: (i % nst, 0)),
        pl.BlockSpec((tm, LANES), lambda i: (i % nst, 0)),
    ]
    feat_spec = pl.BlockSpec((None, 512, tm), lambda i: (i // nst, 0, i % nst))
    tok_spec = pl.BlockSpec((tm, 512), lambda i: (i, 0))
    out_specs = [
        feat_spec,
        tok_spec,
        pl.BlockSpec((None, DIFF_HEADS, V_ROWS, tm), lambda i: (i // nst, 0, 0, i % nst)),
        tok_spec,
        feat_spec,
        tok_spec,
        tok_spec,
    ]
    out_shape = [
        jax.ShapeDtypeStruct((batch, 512, seq), BF16),
        jax.ShapeDtypeStruct((t, 512), BF16),
        jax.ShapeDtypeStruct((batch, DIFF_HEADS, V_ROWS, seq), BF16),
        jax.ShapeDtypeStruct((t, 512), BF16),
        jax.ShapeDtypeStruct((batch, 512, seq), BF16),
        jax.ShapeDtypeStruct((t, 512), BF16),
        jax.ShapeDtypeStruct((t, 512), F32),
    ]
    return pl.pallas_call(
        _inproj_kernel, grid=(t // tm,), in_specs=in_specs, out_specs=out_specs, out_shape=out_shape,
        compiler_params=_cparams(1), name="inproj",
    )(x, ln1, wf, wt, qw_b, kw_b, cosd, sind, cosrt, sinrt, cosr, sinr)


def _attn_kernel(lp_ref, qw_ref, kw_ref, qt_ref, k_ref, v_ref, sw_ref, o_ref, acc1, acc2, m1, m2, l1, l2,
                 *, lam_init, tk, tk_fast):
    seq = k_ref.shape[0]
    qt = qt_ref[...]
    row = lax.broadcasted_iota(jnp.int32, qt.shape, 0)
    zero = jnp.zeros_like(qt)
    qa = jnp.where(row < QK_DIM, qt, zero)
    qb = jnp.where(row >= QK_DIM, qt, zero)
    comps = ((qa, acc1, m1, l1), (qb, acc2, m2, l2))
    lp = lp_ref[...]
    lam = (jnp.exp(jnp.sum(lp[0:1] * lp[1:2], axis=1, keepdims=True))
           - jnp.exp(jnp.sum(lp[2:3] * lp[3:4], axis=1, keepdims=True)) + lam_init)

    def finish():
        out = acc1[...] / l1[...] - lam * (acc2[...] / l2[...])
        oms = jnp.mean(out * out, axis=0, keepdims=True)
        y = out * lax.rsqrt(oms + EPS) * sw_ref[...]
        o_ref[...] = y.T.astype(BF16)

    bound = (jnp.max(jnp.abs(qw_ref[...]), axis=1, keepdims=True)
             * jnp.max(jnp.abs(kw_ref[...]), axis=1, keepdims=True) * (QK_DIM * BOUND_SLACK))
    acc1[...] = jnp.zeros_like(acc1)
    acc2[...] = jnp.zeros_like(acc2)
    l1[...] = jnp.zeros_like(l1)
    l2[...] = jnp.zeros_like(l2)

    def bound_body(j, carry):
        off = pl.multiple_of(j * tk_fast, tk_fast)
        kb = k_ref[pl.ds(off, tk_fast), :]
        vb = v_ref[:, pl.ds(off, tk_fast)]
        for qx, acc, _, l in comps:
            s = jnp.dot(kb, qx, preferred_element_type=F32)
            e = jnp.exp2(s - bound)
            l[...] += jnp.sum(e, axis=0, keepdims=True)
            acc[...] += jnp.dot(vb, e.astype(BF16), preferred_element_type=F32)
        return carry

    lax.fori_loop(0, seq // tk_fast, bound_body, 0)
    l_min = jnp.minimum(jnp.min(l1[...]), jnp.min(l2[...]))
    finish()

    @pl.when(l_min < UNDERFLOW_GUARD)
    def _running_max_pass():
        acc1[...] = jnp.zeros_like(acc1)
        acc2[...] = jnp.zeros_like(acc2)
        m1[...] = jnp.full_like(m1, NEG_BIG)
        m2[...] = jnp.full_like(m2, NEG_BIG)
        l1[...] = jnp.zeros_like(l1)
        l2[...] = jnp.zeros_like(l2)

        def body(j, carry):
            off = pl.multiple_of(j * tk, tk)
            kb = k_ref[pl.ds(off, tk), :]
            vb = v_ref[:, pl.ds(off, tk)]
            for qx, acc, m, l in comps:
                s = jnp.dot(kb, qx, preferred_element_type=F32)
                m_old = m[...]
                m_new = jnp.maximum(m_old, jnp.max(s, axis=0, keepdims=True))
                e = jnp.exp2(s - m_new)
                alpha = jnp.exp2(m_old - m_new)
                l[...] = l[...] * alpha + jnp.sum(e, axis=0, keepdims=True)
                acc[...] = acc[...] * alpha + jnp.dot(vb, e.astype(BF16), preferred_element_type=F32)
                m[...] = m_new
            return carry

        lax.fori_loop(0, seq // tk, body, 0)
        finish()


def _diff_attention(lp, qw, kw, qt, k, v, sw_b, layer, lam_init, batch, seq):
    tq = min(Q_TILE, seq)
    tk = min(KV_TILE, seq)
    tk_fast = min(KV_TILE_FAST, seq)
    k3 = k.reshape(batch, seq, DIFF_WIDTH)
    kern = functools.partial(_attn_kernel, lam_init=lam_init, tk=tk, tk_fast=tk_fast)
    lay3 = lambda b, h, i: (layer, 0, 0)
    return pl.pallas_call(
        kern,
        grid=(batch, DIFF_HEADS, seq // tq),
        in_specs=[
            pl.BlockSpec((None, 4, QK_DIM), lay3),
            pl.BlockSpec((None, 1, QK_DIM), lay3),
            pl.BlockSpec((None, 1, QK_DIM), lay3),
            pl.BlockSpec((None, DV_DIM, tq), lambda b, h, i: (b, h, i)),
            pl.BlockSpec((None, seq, DV_DIM), lambda b, h, i: (b, 0, h)),
            pl.BlockSpec((None, None, V_ROWS, seq), lambda b, h, i: (b, h, 0, 0)),
            pl.BlockSpec((None, DV_DIM, tq), lay3),
        ],
        out_specs=pl.BlockSpec((None, tq, DV_DIM), lambda b, h, i: (b, i, h)),
        out_shape=jax.ShapeDtypeStruct((batch, seq, DIFF_WIDTH), BF16),
        scratch_shapes=[pltpu.VMEM((V_ROWS, tq), F32), pltpu.VMEM((V_ROWS, tq), F32),
                        pltpu.VMEM((1, tq), F32), pltpu.VMEM((1, tq), F32),
                        pltpu.VMEM((1, tq), F32), pltpu.VMEM((1, tq), F32)],
        compiler_params=_cparams(3), name="diff_attention",
    )(lp, qw, kw, qt, k3, v, sw_b)


def _ret_kernel(lgl_ref, lgr_ref, lgs_ref, gnw_ref, q_ref, kt_ref, v_ref, g_ref, o_ref,
                qdf, qdb, kdf, kdb, cdf, cdb, dcomb, gfull, s_st, t_st, t_all, o_buf, *, group):
    b = pl.program_id(0)
    ph = pl.program_id(1)
    j = pl.program_id(2)
    n_groups = pl.num_programs(2)
    c = CHUNK

    @pl.when((b == 0) & (ph == 0) & (j == 0))
    def _tables():
        lgl = jax.nn.log_sigmoid(lgl_ref[...])
        lgr = jax.nn.log_sigmoid(lgr_ref[...])
        lgs = jax.nn.log_sigmoid(lgs_ref[...])
        pos_row = lax.broadcasted_iota(jnp.int32, (c, RET_WIDTH), 0).astype(F32)
        qdf[...] = jnp.exp((pos_row + 1.0) * lgl[0, 0:1, :])
        qdb[...] = jnp.exp((c - pos_row) * lgl[1, 0:1, :])
        pos_lane = lax.broadcasted_iota(jnp.int32, (RET_WIDTH, c), 1).astype(F32)
        kdf[...] = jnp.exp((c - 1.0 - pos_lane) * lgr[0])
        kdb[...] = jnp.exp(pos_lane * lgr[1])
        cdf[...] = jnp.exp(float(c) * lgr[0])
        cdb[...] = jnp.exp(float(c) * lgr[1])
        qi = (lax.broadcasted_iota(jnp.int32, (RET_HEADS * c, c), 0) % c).astype(F32)
        ki = lax.broadcasted_iota(jnp.int32, (RET_HEADS * c, c), 1).astype(F32)
        rel = qi - ki
        dcomb[...] = jnp.where(rel >= 0, jnp.exp(rel * lgs[0]), jnp.exp(-rel * lgs[1]))

        frow = lax.broadcasted_iota(jnp.int32, (RET_WIDTH, RET_WIDTH), 0) // RET_HEAD_DIM
        fcol = lax.broadcasted_iota(jnp.int32, (RET_WIDTH, RET_WIDTH), 1) // RET_HEAD_DIM
        gfull[...] = jnp.where(frow == fcol, 1.0 / RET_HEAD_DIM, 0.0).astype(BF16)

    lane = lax.broadcasted_iota(jnp.int32, (c, c), 1)
    row = lax.broadcasted_iota(jnp.int32, (c, c), 0)
    lo = lane < RET_HEAD_DIM
    same_head = (row < RET_HEAD_DIM) == lo

    @pl.when((ph == 0) & (j == 0))
    def _init():
        t_st[...] = jnp.zeros_like(t_st)
        s_st[...] = jnp.zeros_like(s_st)

    @pl.when(ph == 0)
    def _backward_states():
        first = (n_groups - 1 - j) * group
        for p in range(RET_PAIRS):
            sl = slice(p * c, (p + 1) * c)
            t = t_st[p]
            for g in reversed(range(group)):
                gs = slice(g * c, (g + 1) * c)
                t_all[first + g, p] = t.astype(BF16)
                ktd = (kt_ref[sl, gs].astype(F32) * kdb[sl, :]).astype(BF16)
                kv = jnp.dot(ktd, v_ref[gs, sl], preferred_element_type=F32)
                t = t * cdb[sl, :] + jnp.where(same_head, kv, 0.0)
            t_st[p] = t

    @pl.when(ph == 1)
    def _forward():
        first = j * group
        combos = [(p, g) for p in range(RET_PAIRS) for g in range(group)]
        zq = jnp.zeros((c, c), BF16)

        def blk(p, g):
            return slice(g * c, (g + 1) * c), slice(p * c, (p + 1) * c)

        kvs, sds = {}, {}
        for p, g in combos:
            gs, sl = blk(p, g)
            ktp = kt_ref[sl, gs]
            ktd = (ktp.astype(F32) * kdf[sl, :]).astype(BF16)
            kvs[p, g] = jnp.dot(ktd, v_ref[gs, sl], preferred_element_type=F32)
            qp = q_ref[gs, sl]
            q2 = jnp.concatenate([jnp.where(lo, qp, zq), jnp.where(lo, zq, qp)], axis=0)
            s = jnp.dot(q2, ktp, preferred_element_type=F32)
            sds[p, g] = (s * dcomb[2 * p * c:2 * (p + 1) * c, :]).astype(BF16)
        states = {}
        for p in range(RET_PAIRS):
            sl = slice(p * c, (p + 1) * c)
            s_state = s_st[p]
            for g in range(group):
                states[p, g] = s_state.astype(BF16)
                s_state = s_state * cdf[sl, :] + jnp.where(same_head, kvs[p, g], 0.0)
            s_st[p] = s_state
        for p, g in combos:
            gs, sl = blk(p, g)
            qp = q_ref[gs, sl]
            vp = v_ref[gs, sl]
            qf = qp.astype(F32)
            sd = sds[p, g]
            lhs = jnp.concatenate([sd[0:c], sd[c:2 * c], (qf * qdf[:, sl]).astype(BF16),
                                   (qf * qdb[:, sl]).astype(BF16)], axis=1)
            rhs = jnp.concatenate([jnp.where(lo, vp, zq), jnp.where(lo, zq, vp),
                                   states[p, g], t_all[first + g, p]], axis=0)
            o_buf[gs, sl] = jnp.dot(lhs, rhs, preferred_element_type=F32)
        o = o_buf[...]
        gms = jnp.dot((o * o).astype(BF16), gfull[...], preferred_element_type=F32)
        r = o * lax.rsqrt(gms + EPS) * gnw_ref[0:1, :]
        gate = g_ref[...]
        o_ref[...] = (gate * jax.nn.sigmoid(gate) * r).astype(BF16)


def _retention(lg_lane, lg_row, lg_sc, gnw_b, rq, rkt, rv, rg, layer, batch, seq):
    c = CHUNK
    n = seq // c
    group = min(RET_GROUP, n)
    ng = n // group
    tg = group * c

    def tok_map(b, ph, j):
        return (b * ng + jnp.where(ph == 0, ng - 1 - j, j), 0)

    def feat_map(b, ph, j):
        return (b, 0, jnp.where(ph == 0, ng - 1 - j, j))

    def fwd_only_map(b, ph, j):
        return (b * ng + jnp.where(ph == 0, 0, j), 0)

    return pl.pallas_call(
        functools.partial(_ret_kernel, group=group),
        grid=(batch, 2, ng),
        in_specs=[
            pl.BlockSpec((None, 2, 8, RET_WIDTH), lambda b, ph, j: (layer, 0, 0, 0)),
            pl.BlockSpec((None, 2, RET_WIDTH, c), lambda b, ph, j: (layer, 0, 0, 0)),
            pl.BlockSpec((None, 2, RET_HEADS * c, c), lambda b, ph, j: (layer, 0, 0, 0)),
            pl.BlockSpec((None, 8, RET_WIDTH), lambda b, ph, j: (layer, 0, 0)),
            pl.BlockSpec((tg, RET_WIDTH), fwd_only_map),
            pl.BlockSpec((None, RET_WIDTH, tg), feat_map),
            pl.BlockSpec((tg, RET_WIDTH), tok_map),
            pl.BlockSpec((tg, RET_WIDTH), fwd_only_map),
        ],
        out_specs=pl.BlockSpec((tg, RET_WIDTH), fwd_only_map),
        out_shape=jax.ShapeDtypeStruct((batch * seq, RET_WIDTH), BF16),
        scratch_shapes=[
            pltpu.VMEM((c, RET_WIDTH), F32), pltpu.VMEM((c, RET_WIDTH), F32),
            pltpu.VMEM((RET_WIDTH, c), F32), pltpu.VMEM((RET_WIDTH, c), F32),
            pltpu.VMEM((RET_WIDTH, c), F32), pltpu.VMEM((RET_WIDTH, c), F32),
            pltpu.VMEM((RET_HEADS * c, c), F32),
            pltpu.VMEM((RET_WIDTH, RET_WIDTH), BF16),
            pltpu.VMEM((RET_PAIRS, c, c), F32), pltpu.VMEM((RET_PAIRS, c, c), F32),
            pltpu.VMEM((n, RET_PAIRS, c, c), BF16),
            pltpu.VMEM((tg, RET_WIDTH), F32),
        ],
        compiler_params=_cparams(3), name="retention",
    )(lg_lane, lg_row, lg_sc, gnw_b, rq, rkt, rv, rg)


def _post_kernel(x_ref, a_ref, r_ref, p_ref, wo_ref, ln2_ref, w1_ref, w2_ref, wg_ref, wp_ref, o_ref):
    sub = min(POST_SUB, x_ref.shape[0])
    for part in range(x_ref.shape[0] // sub):
        ts = slice(part * sub, (part + 1) * sub)
        x = x_ref[ts, :]
        x = x + jnp.dot(a_ref[ts, :], wo_ref[0:DIFF_WIDTH, :], preferred_element_type=F32)
        x = x + jnp.dot(r_ref[ts, :], wo_ref[DIFF_WIDTH:, :], preferred_element_type=F32)
        ms = jnp.mean(x * x, axis=-1, keepdims=True)
        h2 = (x * lax.rsqrt(ms + EPS) * ln2_ref[...]).astype(BF16)
        hid = jnp.dot(h2, w1_ref[...], preferred_element_type=F32)
        act = jnp.square(jnp.maximum(hid, 0.0)).astype(BF16)
        x = x + jnp.dot(act, w2_ref[...], preferred_element_type=F32)
        gate = jax.nn.sigmoid(jnp.dot(x.astype(BF16), wg_ref[...], preferred_element_type=F32))
        pe = jnp.dot(p_ref[ts, :].astype(BF16), wp_ref[...], preferred_element_type=F32)
        o_ref[ts, :] = x + gate * pe


def _post(x, a, r, p, wo, ln2, w1, w2, wg, wp, layer, batch, seq):
    t = batch * seq
    tm = min(TOKEN_TILE, seq)
    lay3 = lambda i: (layer, 0, 0)

    def wspec(shape):
        return pl.BlockSpec((None,) + shape, lay3, pipeline_mode=pl.Buffered(1))

    return pl.pallas_call(
        _post_kernel,
        grid=(t // tm,),
        in_specs=[
            pl.BlockSpec((tm, D_MODEL), lambda i: (i, 0)),
            pl.BlockSpec((tm, DIFF_WIDTH), lambda i: (i, 0)),
            pl.BlockSpec((tm, RET_WIDTH), lambda i: (i, 0)),
            pl.BlockSpec((None, tm, PLE_DIM), lambda i: (layer, i, 0)),
            wspec((D_MODEL, D_MODEL)),
            pl.BlockSpec((None, 1, D_MODEL), lay3),
            wspec((D_MODEL, D_FF)),
            wspec((D_FF, D_MODEL)),
            wspec((D_MODEL, D_MODEL)),
            wspec((PLE_DIM, D_MODEL)),
        ],
        out_specs=pl.BlockSpec((tm, D_MODEL), lambda i: (i, 0)),
        out_shape=jax.ShapeDtypeStruct((t, D_MODEL), F32),
        compiler_params=_cparams(1), name="post",
    )(x, a, r, p, wo, ln2, w1, w2, wg, wp)


def _prepare_params(ln1_w, w_in, diff_q_norm, diff_k_norm, diff_subln, ret_decay_logit, ret_gn,
                    w_out, ln2_w, w_mlp1, w_mlp2, w_ple_gate, w_ple_proj):
    depth = w_in.shape[0]
    tm = INPROJ_SUB
    dq, dk, dv = w_in[:, :, 0:512], w_in[:, :, 512:1024], w_in[:, :, 1024:1536]
    rq, rk = w_in[:, :, 1536:2048], w_in[:, :, 2048:2560]
    rv, rg = w_in[:, :, 2560:3072], w_in[:, :, 3072:3584]
    wf = jnp.swapaxes(jnp.concatenate([dq, dk, dv, rk], axis=2), 1, 2).astype(BF16)
    wt = jnp.concatenate([rq, rv, rg], axis=2).astype(BF16)
    q_scale = (QK_DIM ** -0.5) * LOG2E
    qw_b = jnp.broadcast_to((diff_q_norm.astype(F32) * q_scale)[:, :, None], (depth, QK_DIM, tm))
    kw_b = jnp.broadcast_to(diff_k_norm.astype(F32)[:, :, None], (depth, QK_DIM, tm))
    lam_init = [0.8 - 0.6 * math.exp(-0.3 * i) for i in range(depth)]
    sub_scale = jnp.asarray([1.0 - li for li in lam_init], F32)[:, None]
    sw_b = jnp.broadcast_to((diff_subln.astype(F32) * sub_scale)[:, :, None], (depth, DV_DIM, Q_TILE))
    lgt = ret_decay_logit.astype(F32)
    lg_lane = jnp.broadcast_to(jnp.repeat(lgt, RET_HEAD_DIM, axis=2)[:, :, None, :], (depth, 2, 8, RET_WIDTH))
    lg_row = jnp.broadcast_to(jnp.repeat(lgt, RET_HEAD_DIM, axis=2)[:, :, :, None], (depth, 2, RET_WIDTH, CHUNK))
    lg_sc = jnp.broadcast_to(jnp.repeat(lgt, CHUNK, axis=2)[:, :, :, None], (depth, 2, RET_HEADS * CHUNK, CHUNK))
    gnw_b = jnp.broadcast_to(jnp.tile(ret_gn.astype(F32), (1, RET_HEADS))[:, None, :], (depth, 8, RET_WIDTH))
    return dict(
        ln1=ln1_w.astype(F32)[:, None, :], wf=wf, wt=wt, qw_b=qw_b, kw_b=kw_b, sw_b=sw_b, lam_init=lam_init,
        qw=(diff_q_norm.astype(F32) * q_scale)[:, None, :], kw=diff_k_norm.astype(F32)[:, None, :],
        lg_lane=lg_lane, lg_row=lg_row, lg_sc=lg_sc, gnw_b=gnw_b,
        wo=w_out.astype(BF16), ln2=ln2_w.astype(F32)[:, None, :], w1=w_mlp1.astype(BF16),
        w2=w_mlp2.astype(BF16), wg=w_ple_gate.astype(BF16), wp=w_ple_proj.astype(BF16))


def _run_trunk(x, p, prm, tabs, diff_lambda):
    batch, seq, _ = x.shape
    depth = p.shape[0]
    xf = x.reshape(batch * seq, D_MODEL)
    pf = p.reshape(depth, batch * seq, PLE_DIM)
    lp = diff_lambda.astype(F32)
    for i in range(depth):
        qt, k, v, rq, rkt, rv, rg = _inproj(xf, prm["ln1"], prm["wf"], prm["wt"], prm["qw_b"], prm["kw_b"],
                                            tabs, i, batch, seq)
        a = _diff_attention(lp, prm["qw"], prm["kw"], qt, k, v, prm["sw_b"], i, prm["lam_init"][i],
                            batch, seq)
        r = _retention(prm["lg_lane"], prm["lg_row"], prm["lg_sc"], prm["gnw_b"], rq, rkt, rv, rg,
                       i, batch, seq)
        xf = _post(xf, a.reshape(batch * seq, DIFF_WIDTH), r, pf, prm["wo"], prm["ln2"], prm["w1"],
                   prm["w2"], prm["wg"], prm["wp"], i, batch, seq)
    return xf.reshape(batch, seq, D_MODEL)


def kernel(x_prompt, x_sample, p_prompt, p_sample, ln1_w, w_in, diff_q_norm, diff_k_norm, diff_lambda,
           diff_subln, ret_decay_logit, ret_gn, w_out, ln2_w, w_mlp1, w_mlp2, w_ple_gate, w_ple_proj):
    prm = _prepare_params(ln1_w, w_in, diff_q_norm, diff_k_norm, diff_subln, ret_decay_logit, ret_gn,
                          w_out, ln2_w, w_mlp1, w_mlp2, w_ple_gate, w_ple_proj)
    tabs = _rope_tables(max(x_prompt.shape[1], x_sample.shape[1]))
    y_prompt = _run_trunk(x_prompt, p_prompt, prm, tabs, diff_lambda)
    y_sample = _run_trunk(x_sample, p_sample, prm, tabs, diff_lambda)
    return (y_prompt, y_sample)
```

```python
import functools
import math

import jax
import jax.numpy as jnp
from jax import lax
from jax.experimental import pallas as pl
from jax.experimental.pallas import tpu as pltpu

F32 = jnp.float32
BF16 = jnp.bfloat16

D_MODEL = 1024
DIFF_WIDTH = 512
RET_WIDTH = 512
QK_DIM = 64
DV_DIM = 128
DIFF_HEADS = 4
RET_HEAD_DIM = 64
RET_HEADS = 8
RET_PAIRS = RET_HEADS // 2
ROT_DIMS = 16
ROPE_THETA = 500000.0
RET_THETA = 10000.0
D_FF = 4096
PLE_DIM = 256
CHUNK = 128
EPS = 1e-6
LOG2E = 1.4426950408889634
NEG_BIG = -1e30

VMEM_LIMIT_BYTES = 56 * 1024 * 1024
LANES = 128
V_ROWS = DV_DIM

TOKEN_TILE = 1024
POST_SUB = 512
INPROJ_TILE = 1024
INPROJ_SUB = 256
Q_TILE = 1024
KV_TILE = 256
ATTN_SCORE_ELEMS = 4096 * 1024
BOUND_SLACK = 1.02
UNDERFLOW_GUARD = 2.0 ** -64
RET_GROUP = 8
TABLE_TILE = 1024


def _cparams(n_grid):
    return pltpu.CompilerParams(dimension_semantics=("arbitrary",) * n_grid,
                                vmem_limit_bytes=VMEM_LIMIT_BYTES)


def _const_spec(shape):
    nd = len(shape)
    return pl.BlockSpec(shape, lambda *_: (0,) * nd)


def _rope_kernel(invd_ref, invrt_ref, invr_ref, sgn_ref,
                 cosd_ref, sind_ref, cosrt_ref, sinrt_ref, cosr_ref, sinr_ref):
    ts = cosd_ref.shape[1]
    base = pl.program_id(0) * ts
    pos_d = (base + lax.broadcasted_iota(jnp.int32, (8, ts), 1)).astype(F32)
    ang_d = pos_d * invd_ref[...]
    cosd_ref[...] = jnp.cos(ang_d)
    sind_ref[...] = jnp.sin(ang_d)
    pos_r = (base + lax.broadcasted_iota(jnp.int32, (32, ts), 1)).astype(F32)
    ang_r = pos_r * invrt_ref[...]
    cosrt_ref[...] = jnp.cos(ang_r)
    sinrt_ref[...] = jnp.sin(ang_r)
    pos_t = (base + lax.broadcasted_iota(jnp.int32, (ts, LANES), 0)).astype(F32)
    ang_t = pos_t * invr_ref[0:1, :]
    cosr_ref[...] = jnp.cos(ang_t)
    sinr_ref[...] = jnp.sin(ang_t) * sgn_ref[0:1, :]


def _rope_tables(s_max):
    ts = min(TABLE_TILE, s_max)
    inv_d = 1.0 / (ROPE_THETA ** (jnp.arange(0, ROT_DIMS, 2, dtype=F32) / ROT_DIMS))
    inv_r = 1.0 / (RET_THETA ** (jnp.arange(0, RET_HEAD_DIM, 2, dtype=F32) / RET_HEAD_DIM))
    invd_b = jnp.broadcast_to(inv_d[:, None], (8, ts))
    invrt_b = jnp.broadcast_to(inv_r[:, None], (32, ts))
    invr_lane = jnp.broadcast_to(jnp.tile(inv_r, LANES // 32)[None, :], (8, LANES))
    sgn = jnp.where((jnp.arange(LANES) % RET_HEAD_DIM) < RET_HEAD_DIM // 2, -1.0, 1.0).astype(F32)
    sgn_b = jnp.broadcast_to(sgn[None, :], (8, LANES))
    return pl.pallas_call(
        _rope_kernel,
        grid=(s_max // ts,),
        in_specs=[_const_spec((8, ts)), _const_spec((32, ts)), _const_spec((8, LANES)),
                  _const_spec((8, LANES))],
        out_specs=[pl.BlockSpec((8, ts), lambda i: (0, i)), pl.BlockSpec((8, ts), lambda i: (0, i)),
                   pl.BlockSpec((32, ts), lambda i: (0, i)), pl.BlockSpec((32, ts), lambda i: (0, i)),
                   pl.BlockSpec((ts, LANES), lambda i: (i, 0)), pl.BlockSpec((ts, LANES), lambda i: (i, 0))],
        out_shape=[jax.ShapeDtypeStruct((8, s_max), F32), jax.ShapeDtypeStruct((8, s_max), F32),
                   jax.ShapeDtypeStruct((32, s_max), F32), jax.ShapeDtypeStruct((32, s_max), F32),
                   jax.ShapeDtypeStruct((s_max, LANES), F32), jax.ShapeDtypeStruct((s_max, LANES), F32)],
        compiler_params=_cparams(1),
        name="rope_tables",
    )(invd_b, invrt_b, invr_lane, sgn_b)


def _inproj_kernel(x_ref, ln_ref, wf_ref, wt_ref, qw_ref, kw_ref,
                   cd_ref, sd_ref, crt_ref, srt_ref, cr_ref, sr_ref,
                   qt_ref, k_ref, v_ref, rq_ref, rkt_ref, rv_ref, rg_ref):
    tm = min(INPROJ_SUB, x_ref.shape[0])
    half = RET_HEAD_DIM // 2

    def norm_rot(z, w, cd, sd):
        z3 = z.reshape(2 * DIFF_HEADS, QK_DIM, tm)
        gms = jnp.mean(z3 * z3, axis=1, keepdims=True)
        y = z3 * lax.rsqrt(gms + EPS) * w[None]
        y1 = y[:, 0:8]
        y2 = y[:, 8:16]
        out = jnp.concatenate([y1 * cd - y2 * sd, y1 * sd + y2 * cd, y[:, 16:]], axis=1)
        return out.reshape(DIFF_WIDTH, tm)

    for part in range(x_ref.shape[0] // tm):
        ts = slice(part * tm, (part + 1) * tm)
        x = x_ref[ts, :]
        ms = jnp.mean(x * x, axis=-1, keepdims=True)
        hn = (x * lax.rsqrt(ms + EPS) * ln_ref[...]).astype(BF16)

        def feat_dot(lo, hi):
            return lax.dot_general(wf_ref[lo:hi, :], hn, (((1,), (1,)), ((), ())),
                                   preferred_element_type=F32)

        def tok_dot(lo, hi):
            return jnp.dot(hn, wt_ref[:, lo:hi], preferred_element_type=F32)

        cd = cd_ref[:, ts]
        sd = sd_ref[:, ts]
        k_ref[ts, :] = norm_rot(feat_dot(512, 1024), kw_ref[...], cd, sd).T.astype(BF16)
        qt_ref[:, ts] = norm_rot(feat_dot(0, 512), qw_ref[...], cd, sd).astype(BF16)

        rq = tok_dot(0, 512)
        lane = lax.broadcasted_iota(jnp.int32, rq.shape, 1)
        partner = jnp.where((lane % RET_HEAD_DIM) < half,
                            pltpu.roll(rq, RET_WIDTH - half, 1), pltpu.roll(rq, half, 1))
        cr = jnp.concatenate([cr_ref[ts, :]] * (RET_WIDTH // LANES), axis=1)
        sr = jnp.concatenate([sr_ref[ts, :]] * (RET_WIDTH // LANES), axis=1)
        rq_ref[ts, :] = (rq * cr + partner * sr).astype(BF16)

        rk3 = feat_dot(1536, 2048).reshape(RET_HEADS, RET_HEAD_DIM, tm)
        crt = crt_ref[:, ts]
        srt = srt_ref[:, ts]
        r1 = rk3[:, 0:half]
        r2 = rk3[:, half:]
        rk_rot = jnp.concatenate([r1 * crt - r2 * srt, r1 * srt + r2 * crt], axis=1)
        rkt_ref[:, ts] = (rk_rot * (RET_HEAD_DIM ** -0.5)).reshape(RET_WIDTH, tm).astype(BF16)

        v_ref[:, :, ts] = feat_dot(1024, 1536).reshape(DIFF_HEADS, DV_DIM, tm).astype(BF16)
        rv_ref[ts, :] = tok_dot(512, 1024).astype(BF16)
        rg_ref[ts, :] = tok_dot(1024, 1536)


def _inproj(x, ln1, wf, wt, qw_b, kw_b, tabs, layer, batch, seq):
    t = batch * seq
    tm = min(INPROJ_TILE, seq)
    nst = seq // tm
    cosd, sind, cosrt, sinrt, cosr, sinr = tabs
    lay3 = lambda i: (layer, 0, 0)
    in_specs = [
        pl.BlockSpec((tm, D_MODEL), lambda i: (i, 0)),
        pl.BlockSpec((None, 1, D_MODEL), lay3),
        pl.BlockSpec((None, 2048, D_MODEL), lay3),
        pl.BlockSpec((None, D_MODEL, 1536), lay3),
        pl.BlockSpec((None, QK_DIM, min(INPROJ_SUB, tm)), lay3),
        pl.BlockSpec((None, QK_DIM, min(INPROJ_SUB, tm)), lay3),
        pl.BlockSpec((8, tm), lambda i: (0, i % nst)),
        pl.BlockSpec((8, tm), lambda i: (0, i % nst)),
        pl.BlockSpec((32, tm), lambda i: (0, i % nst)),
        pl.BlockSpec((32, tm), lambda i: (0, i % nst)),
        pl.BlockSpec((tm, LANES), lambda i: (i % nst, 0)),
        pl.BlockSpec((tm, LANES), lambda i: (i % nst, 0)),
    ]
    feat_spec = pl.BlockSpec((None, 512, tm), lambda i: (i // nst, 0, i % nst))
    tok_spec = pl.BlockSpec((tm, 512), lambda i: (i, 0))
    out_specs = [
        feat_spec,
        tok_spec,
        pl.BlockSpec((None, DIFF_HEADS, V_ROWS, tm), lambda i: (i // nst, 0, 0, i % nst)),
        tok_spec,
        feat_spec,
        tok_spec,
        tok_spec,
    ]
    out_shape = [
        jax.ShapeDtypeStruct((batch, 512, seq), BF16),
        jax.ShapeDtypeStruct((t, 512), BF16),
        jax.ShapeDtypeStruct((batch, DIFF_HEADS, V_ROWS, seq), BF16),
        jax.ShapeDtypeStruct((t, 512), BF16),
        jax.ShapeDtypeStruct((batch, 512, seq), BF16),
        jax.ShapeDtypeStruct((t, 512), BF16),
        jax.ShapeDtypeStruct((t, 512), F32),
    ]
    return pl.pallas_call(
        _inproj_kernel, grid=(t // tm,), in_specs=in_specs, out_specs=out_specs, out_shape=out_shape,
        compiler_params=_cparams(1), name="inproj",
    )(x, ln1, wf, wt, qw_b, kw_b, cosd, sind, cosrt, sinrt, cosr, sinr)


def _attn_kernel(lp_ref, qw_ref, kw_ref, qt_ref, k_ref, v_ref, sw_ref, o_ref,
                 acc1, acc2, m1, m2, l1, l2, p_buf, *, lam_init, tk):
    seq = k_ref.shape[0]
    qt = qt_ref[...]
    row = lax.broadcasted_iota(jnp.int32, qt.shape, 0)
    zero = jnp.zeros_like(qt)
    qa = jnp.where(row < QK_DIM, qt, zero)
    qb = jnp.where(row >= QK_DIM, qt, zero)
    lp = lp_ref[...]
    lam = (jnp.exp(jnp.sum(lp[0:1] * lp[1:2], axis=1, keepdims=True))
           - jnp.exp(jnp.sum(lp[2:3] * lp[3:4], axis=1, keepdims=True)) + lam_init)

    def finish(out):
        oms = jnp.mean(out * out, axis=0, keepdims=True)
        y = out * lax.rsqrt(oms + EPS) * sw_ref[...]
        o_ref[...] = y.T.astype(BF16)

    bound = (jnp.max(jnp.abs(qw_ref[...]), axis=1, keepdims=True)
             * jnp.max(jnp.abs(kw_ref[...]), axis=1, keepdims=True) * (QK_DIM * BOUND_SLACK))
    kb = k_ref[...]
    row_sums = []
    for c, qx in enumerate((qa, qb)):
        e = jnp.exp2(jnp.dot(kb, qx, preferred_element_type=F32) - bound)
        row_sums.append(jnp.sum(e, axis=0, keepdims=True))
        p_buf[c] = e.astype(BF16)
    s1, s2 = row_sums
    ratio = (lam * s1 / s2).astype(BF16)
    w = p_buf[0] - ratio * p_buf[1]
    finish(jnp.dot(v_ref[...], w, preferred_element_type=F32) / s1)
    l_min = jnp.minimum(jnp.min(s1), jnp.min(s2))

    @pl.when(l_min < UNDERFLOW_GUARD)
    def _running_max_pass():
        comps = ((qa, acc1, m1, l1), (qb, acc2, m2, l2))
        for _, acc, m, l in comps:
            acc[...] = jnp.zeros_like(acc)
            l[...] = jnp.zeros_like(l)
            m[...] = jnp.full_like(m, NEG_BIG)

        def body(j, carry):
            off = pl.multiple_of(j * tk, tk)
            kblk = k_ref[pl.ds(off, tk), :]
            vblk = v_ref[:, pl.ds(off, tk)]
            for qx, acc, m, l in comps:
                s = jnp.dot(kblk, qx, preferred_element_type=F32)
                m_old = m[...]
                m_new = jnp.maximum(m_old, jnp.max(s, axis=0, keepdims=True))
                e = jnp.exp2(s - m_new)
                alpha = jnp.exp2(m_old - m_new)
                l[...] = l[...] * alpha + jnp.sum(e, axis=0, keepdims=True)
                acc[...] = acc[...] * alpha + jnp.dot(vblk, e.astype(BF16), preferred_element_type=F32)
                m[...] = m_new
            return carry

        lax.fori_loop(0, seq // tk, body, 0)
        finish(acc1[...] / l1[...] - lam * (acc2[...] / l2[...]))


def _diff_attention(lp, qw, kw, qt, k, v, sw_b, layer, lam_init, batch, seq):
    tq = min(Q_TILE, seq, ATTN_SCORE_ELEMS // seq)
    tk = min(KV_TILE, seq)
    k3 = k.reshape(batch, seq, DIFF_WIDTH)
    kern = functools.partial(_attn_kernel, lam_init=lam_init, tk=tk)
    lay3 = lambda b, h, i: (layer, 0, 0)
    return pl.pallas_call(
        kern,
        grid=(batch, DIFF_HEADS, seq // tq),
        in_specs=[
            pl.BlockSpec((None, 4, QK_DIM), lay3),
            pl.BlockSpec((None, 1, QK_DIM), lay3),
            pl.BlockSpec((None, 1, QK_DIM), lay3),
            pl.BlockSpec((None, DV_DIM, tq), lambda b, h, i: (b, h, i)),
            pl.BlockSpec((None, seq, DV_DIM), lambda b, h, i: (b, 0, h)),
            pl.BlockSpec((None, None, V_ROWS, seq), lambda b, h, i: (b, h, 0, 0)),
            pl.BlockSpec((None, DV_DIM, tq), lay3),
        ],
        out_specs=pl.BlockSpec((None, tq, DV_DIM), lambda b, h, i: (b, i, h)),
        out_shape=jax.ShapeDtypeStruct((batch, seq, DIFF_WIDTH), BF16),
        scratch_shapes=[pltpu.VMEM((V_ROWS, tq), F32), pltpu.VMEM((V_ROWS, tq), F32),
                        pltpu.VMEM((1, tq), F32), pltpu.VMEM((1, tq), F32),
                        pltpu.VMEM((1, tq), F32), pltpu.VMEM((1, tq), F32),
                        pltpu.VMEM((2, seq, tq), BF16)],
        compiler_params=_cparams(3), name="diff_attention",
    )(lp, qw, kw, qt, k3, v, sw_b)


def _ret_kernel(lgl_ref, lgr_ref, lgs_ref, gnw_ref, q_ref, kt_ref, v_ref, g_ref, o_ref,
                qdf, qdb, kdf, kdb, cdf, cdb, dcomb, gfull, s_st, t_st, t_all, o_buf, *, group):
    b = pl.program_id(0)
    ph = pl.program_id(1)
    j = pl.program_id(2)
    n_groups = pl.num_programs(2)
    c = CHUNK

    @pl.when((b == 0) & (ph == 0) & (j == 0))
    def _tables():
        lgl = jax.nn.log_sigmoid(lgl_ref[...])
        lgr = jax.nn.log_sigmoid(lgr_ref[...])
        lgs = jax.nn.log_sigmoid(lgs_ref[...])
        pos_row = lax.broadcasted_iota(jnp.int32, (c, RET_WIDTH), 0).astype(F32)
        qdf[...] = jnp.exp((pos_row + 1.0) * lgl[0, 0:1, :])
        qdb[...] = jnp.exp((c - pos_row) * lgl[1, 0:1, :])
        pos_lane = lax.broadcasted_iota(jnp.int32, (RET_WIDTH, c), 1).astype(F32)
        kdf[...] = jnp.exp((c - 1.0 - pos_lane) * lgr[0])
        kdb[...] = jnp.exp(pos_lane * lgr[1])
        cdf[...] = jnp.exp(float(c) * lgr[0])
        cdb[...] = jnp.exp(float(c) * lgr[1])
        qi = (lax.broadcasted_iota(jnp.int32, (RET_HEADS * c, c), 0) % c).astype(F32)
        ki = lax.broadcasted_iota(jnp.int32, (RET_HEADS * c, c), 1).astype(F32)
        rel = qi - ki
        dcomb[...] = jnp.where(rel >= 0, jnp.exp(rel * lgs[0]), jnp.exp(-rel * lgs[1]))

        frow = lax.broadcasted_iota(jnp.int32, (RET_WIDTH, RET_WIDTH), 0) // RET_HEAD_DIM
        fcol = lax.broadcasted_iota(jnp.int32, (RET_WIDTH, RET_WIDTH), 1) // RET_HEAD_DIM
        gfull[...] = jnp.where(frow == fcol, 1.0 / RET_HEAD_DIM, 0.0).astype(BF16)

    lane = lax.broadcasted_iota(jnp.int32, (c, c), 1)
    row = lax.broadcasted_iota(jnp.int32, (c, c), 0)
    lo = lane < RET_HEAD_DIM
    same_head = (row < RET_HEAD_DIM) == lo

    @pl.when((ph == 0) & (j == 0))
    def _init():
        t_st[...] = jnp.zeros_like(t_st)
        s_st[...] = jnp.zeros_like(s_st)

    @pl.when(ph == 0)
    def _backward_states():
        first = (n_groups - 1 - j) * group
        for p in range(RET_PAIRS):
            sl = slice(p * c, (p + 1) * c)
            t = t_st[p]
            for g in reversed(range(group)):
                gs = slice(g * c, (g + 1) * c)
                t_all[first + g, p] = t.astype(BF16)
                ktd = (kt_ref[sl, gs].astype(F32) * kdb[sl, :]).astype(BF16)
                kv = jnp.dot(ktd, v_ref[gs, sl], preferred_element_type=F32)
                t = t * cdb[sl, :] + jnp.where(same_head, kv, 0.0)
            t_st[p] = t

    @pl.when(ph == 1)
    def _forward():
        first = j * group
        combos = [(p, g) for p in range(RET_PAIRS) for g in range(group)]
        zq = jnp.zeros((c, c), BF16)

        def blk(p, g):
            return slice(g * c, (g + 1) * c), slice(p * c, (p + 1) * c)

        kvs, sds = {}, {}
        for p, g in combos:
            gs, sl = blk(p, g)
            ktp = kt_ref[sl, gs]
            ktd = (ktp.astype(F32) * kdf[sl, :]).astype(BF16)
            kvs[p, g] = jnp.dot(ktd, v_ref[gs, sl], preferred_element_type=F32)
            qp = q_ref[gs, sl]
            q2 = jnp.concatenate([jnp.where(lo, qp, zq), jnp.where(lo, zq, qp)], axis=0)
            s = jnp.dot(q2, ktp, preferred_element_type=F32)
            sds[p, g] = (s * dcomb[2 * p * c:2 * (p + 1) * c, :]).astype(BF16)
        states = {}
        for p in range(RET_PAIRS):
            sl = slice(p * c, (p + 1) * c)
            s_state = s_st[p]
            for g in range(group):
                states[p, g] = s_state.astype(BF16)
                s_state = s_state * cdf[sl, :] + jnp.where(same_head, kvs[p, g], 0.0)
            s_st[p] = s_state
        for p, g in combos:
            gs, sl = blk(p, g)
            qp = q_ref[gs, sl]
            vp = v_ref[gs, sl]
            qf = qp.astype(F32)
            sd = sds[p, g]
            lhs = jnp.concatenate([sd[0:c], sd[c:2 * c], (qf * qdf[:, sl]).astype(BF16),
                                   (qf * qdb[:, sl]).astype(BF16)], axis=1)
            rhs = jnp.concatenate([jnp.where(lo, vp, zq), jnp.where(lo, zq, vp),
                                   states[p, g], t_all[first + g, p]], axis=0)
            o_buf[gs, sl] = jnp.dot(lhs, rhs, preferred_element_type=F32)
        o = o_buf[...]
        gms = jnp.dot((o * o).astype(BF16), gfull[...], preferred_element_type=F32)
        r = o * lax.rsqrt(gms + EPS) * gnw_ref[0:1, :]
        gate = g_ref[...]
        o_ref[...] = (gate * jax.nn.sigmoid(gate) * r).astype(BF16)


def _retention(lg_lane, lg_row, lg_sc, gnw_b, rq, rkt, rv, rg, layer, batch, seq):
    c = CHUNK
    n = seq // c
    group = min(RET_GROUP, n)
    ng = n // group
    tg = group * c

    def tok_map(b, ph, j):
        return (b * ng + jnp.where(ph == 0, ng - 1 - j, j), 0)

    def feat_map(b, ph, j):
        return (b, 0, jnp.where(ph == 0, ng - 1 - j, j))

    def fwd_only_map(b, ph, j):
        return (b * ng + jnp.where(ph == 0, 0, j), 0)

    return pl.pallas_call(
        functools.partial(_ret_kernel, group=group),
        grid=(batch, 2, ng),
        in_specs=[
            pl.BlockSpec((None, 2, 8, RET_WIDTH), lambda b, ph, j: (layer, 0, 0, 0)),
            pl.BlockSpec((None, 2, RET_WIDTH, c), lambda b, ph, j: (layer, 0, 0, 0)),
            pl.BlockSpec((None, 2, RET_HEADS * c, c), lambda b, ph, j: (layer, 0, 0, 0)),
            pl.BlockSpec((None, 8, RET_WIDTH), lambda b, ph, j: (layer, 0, 0)),
            pl.BlockSpec((tg, RET_WIDTH), fwd_only_map),
            pl.BlockSpec((None, RET_WIDTH, tg), feat_map),
            pl.BlockSpec((tg, RET_WIDTH), tok_map),
            pl.BlockSpec((tg, RET_WIDTH), fwd_only_map),
        ],
        out_specs=pl.BlockSpec((tg, RET_WIDTH), fwd_only_map),
        out_shape=jax.ShapeDtypeStruct((batch * seq, RET_WIDTH), BF16),
        scratch_shapes=[
            pltpu.VMEM((c, RET_WIDTH), F32), pltpu.VMEM((c, RET_WIDTH), F32),
            pltpu.VMEM((RET_WIDTH, c), F32), pltpu.VMEM((RET_WIDTH, c), F32),
            pltpu.VMEM((RET_WIDTH, c), F32), pltpu.VMEM((RET_WIDTH, c), F32),
            pltpu.VMEM((RET_HEADS * c, c), F32),
            pltpu.VMEM((RET_WIDTH, RET_WIDTH), BF16),
            pltpu.VMEM((RET_PAIRS, c, c), F32), pltpu.VMEM((RET_PAIRS, c, c), F32),
            pltpu.VMEM((n, RET_PAIRS, c, c), BF16),
            pltpu.VMEM((tg, RET_WIDTH), F32),
        ],
        compiler_params=_cparams(3), name="retention",
    )(lg_lane, lg_row, lg_sc, gnw_b, rq, rkt, rv, rg)


def _post_kernel(x_ref, a_ref, r_ref, p_ref, wo_ref, ln2_ref, w1_ref, w2_ref, wg_ref, wp_ref, o_ref):
    sub = min(POST_SUB, x_ref.shape[0])
    for part in range(x_ref.shape[0] // sub):
        ts = slice(part * sub, (part + 1) * sub)
        x = x_ref[ts, :]
        x = x + jnp.dot(a_ref[ts, :], wo_ref[0:DIFF_WIDTH, :], preferred_element_type=F32)
        x = x + jnp.dot(r_ref[ts, :], wo_ref[DIFF_WIDTH:, :], preferred_element_type=F32)
        ms = jnp.mean(x * x, axis=-1, keepdims=True)
        h2 = (x * lax.rsqrt(ms + EPS) * ln2_ref[...]).astype(BF16)
        hid = jnp.dot(h2, w1_ref[...], preferred_element_type=F32)
        act = jnp.square(jnp.maximum(hid, 0.0)).astype(BF16)
        x = x + jnp.dot(act, w2_ref[...], preferred_element_type=F32)
        gate = jax.nn.sigmoid(jnp.dot(x.astype(BF16), wg_ref[...], preferred_element_type=F32))
        pe = jnp.dot(p_ref[ts, :].astype(BF16), wp_ref[...], preferred_element_type=F32)
        o_ref[ts, :] = x + gate * pe


def _post(x, a, r, p, wo, ln2, w1, w2, wg, wp, layer, batch, seq):
    t = batch * seq
    tm = min(TOKEN_TILE, seq)
    lay3 = lambda i: (layer, 0, 0)

    def wspec(shape):
        return pl.BlockSpec((None,) + shape, lay3, pipeline_mode=pl.Buffered(1))

    return pl.pallas_call(
        _post_kernel,
        grid=(t // tm,),
        in_specs=[
            pl.BlockSpec((tm, D_MODEL), lambda i: (i, 0)),
            pl.BlockSpec((tm, DIFF_WIDTH), lambda i: (i, 0)),
            pl.BlockSpec((tm, RET_WIDTH), lambda i: (i, 0)),
            pl.BlockSpec((None, tm, PLE_DIM), lambda i: (layer, i, 0)),
            wspec((D_MODEL, D_MODEL)),
            pl.BlockSpec((None, 1, D_MODEL), lay3),
            wspec((D_MODEL, D_FF)),
            wspec((D_FF, D_MODEL)),
            wspec((D_MODEL, D_MODEL)),
            wspec((PLE_DIM, D_MODEL)),
        ],
        out_specs=pl.BlockSpec((tm, D_MODEL), lambda i: (i, 0)),
        out_shape=jax.ShapeDtypeStruct((t, D_MODEL), F32),
        compiler_params=_cparams(1), name="post",
    )(x, a, r, p, wo, ln2, w1, w2, wg, wp)


def _prepare_params(ln1_w, w_in, diff_q_norm, diff_k_norm, diff_subln, ret_decay_logit, ret_gn,
                    w_out, ln2_w, w_mlp1, w_mlp2, w_ple_gate, w_ple_proj):
    depth = w_in.shape[0]
    tm = INPROJ_SUB
    dq, dk, dv = w_in[:, :, 0:512], w_in[:, :, 512:1024], w_in[:, :, 1024:1536]
    rq, rk = w_in[:, :, 1536:2048], w_in[:, :, 2048:2560]
    rv, rg = w_in[:, :, 2560:3072], w_in[:, :, 3072:3584]
    wf = jnp.swapaxes(jnp.concatenate([dq, dk, dv, rk], axis=2), 1, 2).astype(BF16)
    wt = jnp.concatenate([rq, rv, rg], axis=2).astype(BF16)
    q_scale = (QK_DIM ** -0.5) * LOG2E
    qw_b = jnp.broadcast_to((diff_q_norm.astype(F32) * q_scale)[:, :, None], (depth, QK_DIM, tm))
    kw_b = jnp.broadcast_to(diff_k_norm.astype(F32)[:, :, None], (depth, QK_DIM, tm))
    lam_init = [0.8 - 0.6 * math.exp(-0.3 * i) for i in range(depth)]
    sub_scale = jnp.asarray([1.0 - li for li in lam_init], F32)[:, None]
    sw_b = jnp.broadcast_to((diff_subln.astype(F32) * sub_scale)[:, :, None], (depth, DV_DIM, Q_TILE))
    lgt = ret_decay_logit.astype(F32)
    lg_lane = jnp.broadcast_to(jnp.repeat(lgt, RET_HEAD_DIM, axis=2)[:, :, None, :], (depth, 2, 8, RET_WIDTH))
    lg_row = jnp.broadcast_to(jnp.repeat(lgt, RET_HEAD_DIM, axis=2)[:, :, :, None], (depth, 2, RET_WIDTH, CHUNK))
    lg_sc = jnp.broadcast_to(jnp.repeat(lgt, CHUNK, axis=2)[:, :, :, None], (depth, 2, RET_HEADS * CHUNK, CHUNK))
    gnw_b = jnp.broadcast_to(jnp.tile(ret_gn.astype(F32), (1, RET_HEADS))[:, None, :], (depth, 8, RET_WIDTH))
    return dict(
        ln1=ln1_w.astype(F32)[:, None, :], wf=wf, wt=wt, qw_b=qw_b, kw_b=kw_b, sw_b=sw_b, lam_init=lam_init,
        qw=(diff_q_norm.astype(F32) * q_scale)[:, None, :], kw=diff_k_norm.astype(F32)[:, None, :],
        lg_lane=lg_lane, lg_row=lg_row, lg_sc=lg_sc, gnw_b=gnw_b,
        wo=w_out.astype(BF16), ln2=ln2_w.astype(F32)[:, None, :], w1=w_mlp1.astype(BF16),
        w2=w_mlp2.astype(BF16), wg=w_ple_gate.astype(BF16), wp=w_ple_proj.astype(BF16))


def _run_trunk(x, p, prm, tabs, diff_lambda):
    batch, seq, _ = x.shape
    depth = p.shape[0]
    xf = x.reshape(batch * seq, D_MODEL)
    pf = p.reshape(depth, batch * seq, PLE_DIM)
    lp = diff_lambda.astype(F32)
    for i in range(depth):
        qt, k, v, rq, rkt, rv, rg = _inproj(xf, prm["ln1"], prm["wf"], prm["wt"], prm["qw_b"], prm["kw_b"],
                                            tabs, i, batch, seq)
        a = _diff_attention(lp, prm["qw"], prm["kw"], qt, k, v, prm["sw_b"], i, prm["lam_init"][i],
                            batch, seq)
        r = _retention(prm["lg_lane"], prm["lg_row"], prm["lg_sc"], prm["gnw_b"], rq, rkt, rv, rg,
                       i, batch, seq)
        xf = _post(xf, a.reshape(batch * seq, DIFF_WIDTH), r, pf, prm["wo"], prm["ln2"], prm["w1"],
                   prm["w2"], prm["wg"], prm["wp"], i, batch, seq)
    return xf.reshape(batch, seq, D_MODEL)


def kernel(x_prompt, x_sample, p_prompt, p_sample, ln1_w, w_in, diff_q_norm, diff_k_norm, diff_lambda,
           diff_subln, ret_decay_logit, ret_gn, w_out, ln2_w, w_mlp1, w_mlp2, w_ple_gate, w_ple_proj):
    prm = _prepare_params(ln1_w, w_in, diff_q_norm, diff_k_norm, diff_subln, ret_decay_logit, ret_gn,
                          w_out, ln2_w, w_mlp1, w_mlp2, w_ple_gate, w_ple_proj)
    tabs = _rope_tables(max(x_prompt.shape[1], x_sample.shape[1]))
    y_prompt = _run_trunk(x_prompt, p_prompt, prm, tabs, diff_lambda)
    y_sample = _run_trunk(x_sample, p_sample, prm, tabs, diff_lambda)
    return (y_prompt, y_sample)
```

```python
import functools
import math

import jax
import jax.numpy as jnp
from jax import lax
from jax.experimental import pallas as pl
from jax.experimental.pallas import tpu as pltpu

F32 = jnp.float32
BF16 = jnp.bfloat16

D_MODEL = 1024
DIFF_WIDTH = 512
RET_WIDTH = 512
QK_DIM = 64
DV_DIM = 128
DIFF_HEADS = 4
RET_HEAD_DIM = 64
RET_HEADS = 8
RET_PAIRS = RET_HEADS // 2
ROT_DIMS = 16
ROPE_THETA = 500000.0
RET_THETA = 10000.0
D_FF = 4096
PLE_DIM = 256
CHUNK = 128
EPS = 1e-6
LOG2E = 1.4426950408889634
NEG_BIG = -1e30

V7X_VMEM_BYTES = 64 * 1024 * 1024
VMEM_LIMIT_BYTES = V7X_VMEM_BYTES - 8 * 1024 * 1024
LANES = 128

FEAT_DQ, FEAT_DK, FEAT_DV, FEAT_RK = ((i * 512, (i + 1) * 512) for i in range(4))
TOK_RQ, TOK_RV, TOK_RG = ((i * 512, (i + 1) * 512) for i in range(3))
FEAT_ROWS = FEAT_RK[1]
TOK_COLS = TOK_RG[1]

TOKEN_TILE = 1024
POST_SUB = 512
INPROJ_TILE = 1024
INPROJ_SUB = 256
Q_TILE = 1024
KV_TILE = 256
ATTN_SCORE_ELEMS = 4096 * 1024
BOUND_SLACK = 1.02
UNDERFLOW_GUARD = 2.0 ** -64
RET_GROUP = 8
TABLE_TILE = 1024


def _cparams(n_grid):
    return pltpu.CompilerParams(dimension_semantics=("arbitrary",) * n_grid,
                                vmem_limit_bytes=VMEM_LIMIT_BYTES)


def _const_spec(shape):
    nd = len(shape)
    return pl.BlockSpec(shape, lambda *_: (0,) * nd)


def _rope_kernel(invd_ref, invrt_ref, invr_ref, sgn_ref,
                 cosd_ref, sind_ref, cosrt_ref, sinrt_ref, cosr_ref, sinr_ref):
    ts = cosd_ref.shape[1]
    base = pl.program_id(0) * ts
    pos_d = (base + lax.broadcasted_iota(jnp.int32, (8, ts), 1)).astype(F32)
    ang_d = pos_d * invd_ref[...]
    cosd_ref[...] = jnp.cos(ang_d)
    sind_ref[...] = jnp.sin(ang_d)
    pos_r = (base + lax.broadcasted_iota(jnp.int32, (32, ts), 1)).astype(F32)
    ang_r = pos_r * invrt_ref[...]
    cosrt_ref[...] = jnp.cos(ang_r)
    sinrt_ref[...] = jnp.sin(ang_r)
    pos_t = (base + lax.broadcasted_iota(jnp.int32, (ts, LANES), 0)).astype(F32)
    ang_t = pos_t * invr_ref[0:1, :]
    cosr_ref[...] = jnp.cos(ang_t)
    sinr_ref[...] = jnp.sin(ang_t) * sgn_ref[0:1, :]


def _rope_tables(s_max):
    ts = min(TABLE_TILE, s_max)
    inv_d = 1.0 / (ROPE_THETA ** (jnp.arange(0, ROT_DIMS, 2, dtype=F32) / ROT_DIMS))
    inv_r = 1.0 / (RET_THETA ** (jnp.arange(0, RET_HEAD_DIM, 2, dtype=F32) / RET_HEAD_DIM))
    invd_b = jnp.broadcast_to(inv_d[:, None], (8, ts))
    invrt_b = jnp.broadcast_to(inv_r[:, None], (32, ts))
    invr_lane = jnp.broadcast_to(jnp.tile(inv_r, LANES // 32)[None, :], (8, LANES))
    sgn = jnp.where((jnp.arange(LANES) % RET_HEAD_DIM) < RET_HEAD_DIM // 2, -1.0, 1.0).astype(F32)
    sgn_b = jnp.broadcast_to(sgn[None, :], (8, LANES))
    return pl.pallas_call(
        _rope_kernel,
        grid=(s_max // ts,),
        in_specs=[_const_spec((8, ts)), _const_spec((32, ts)), _const_spec((8, LANES)),
                  _const_spec((8, LANES))],
        out_specs=[pl.BlockSpec((8, ts), lambda i: (0, i)), pl.BlockSpec((8, ts), lambda i: (0, i)),
                   pl.BlockSpec((32, ts), lambda i: (0, i)), pl.BlockSpec((32, ts), lambda i: (0, i)),
                   pl.BlockSpec((ts, LANES), lambda i: (i, 0)), pl.BlockSpec((ts, LANES), lambda i: (i, 0))],
        out_shape=[jax.ShapeDtypeStruct((8, s_max), F32), jax.ShapeDtypeStruct((8, s_max), F32),
                   jax.ShapeDtypeStruct((32, s_max), F32), jax.ShapeDtypeStruct((32, s_max), F32),
                   jax.ShapeDtypeStruct((s_max, LANES), F32), jax.ShapeDtypeStruct((s_max, LANES), F32)],
        compiler_params=_cparams(1),
        name="rope_tables",
    )(invd_b, invrt_b, invr_lane, sgn_b)


def _inproj_kernel(x_ref, ln_ref, wf_ref, wt_ref, qw_ref, kw_ref,
                   cd_ref, sd_ref, crt_ref, srt_ref, cr_ref, sr_ref,
                   qt_ref, k_ref, v_ref, rq_ref, rkt_ref, rv_ref, rg_ref):
    tm = min(INPROJ_SUB, x_ref.shape[0])
    half = RET_HEAD_DIM // 2

    def norm_rot(z, w, cd, sd):
        z3 = z.reshape(2 * DIFF_HEADS, QK_DIM, tm)
        gms = jnp.mean(z3 * z3, axis=1, keepdims=True)
        y = z3 * lax.rsqrt(gms + EPS) * w[None]
        y1 = y[:, 0:8]
        y2 = y[:, 8:16]
        out = jnp.concatenate([y1 * cd - y2 * sd, y1 * sd + y2 * cd, y[:, 16:]], axis=1)
        return out.reshape(DIFF_WIDTH, tm)

    for part in range(x_ref.shape[0] // tm):
        ts = slice(part * tm, (part + 1) * tm)
        x = x_ref[ts, :]
        ms = jnp.mean(x * x, axis=-1, keepdims=True)
        hn = (x * lax.rsqrt(ms + EPS) * ln_ref[...]).astype(BF16)

        def feat_dot(lo, hi):
            return lax.dot_general(wf_ref[lo:hi, :], hn, (((1,), (1,)), ((), ())),
                                   preferred_element_type=F32)

        def tok_dot(lo, hi):
            return jnp.dot(hn, wt_ref[:, lo:hi], preferred_element_type=F32)

        cd = cd_ref[:, ts]
        sd = sd_ref[:, ts]
        k_ref[ts, :] = norm_rot(feat_dot(*FEAT_DK), kw_ref[...], cd, sd).T.astype(BF16)
        qt_ref[:, ts] = norm_rot(feat_dot(*FEAT_DQ), qw_ref[...], cd, sd).astype(BF16)

        rq = tok_dot(*TOK_RQ)
        lane = lax.broadcasted_iota(jnp.int32, rq.shape, 1)
        partner = jnp.where((lane % RET_HEAD_DIM) < half,
                            pltpu.roll(rq, RET_WIDTH - half, 1), pltpu.roll(rq, half, 1))
        cr = jnp.concatenate([cr_ref[ts, :]] * (RET_WIDTH // LANES), axis=1)
        sr = jnp.concatenate([sr_ref[ts, :]] * (RET_WIDTH // LANES), axis=1)
        rq_ref[ts, :] = (rq * cr + partner * sr).astype(BF16)

        rk3 = feat_dot(*FEAT_RK).reshape(RET_HEADS, RET_HEAD_DIM, tm)
        crt = crt_ref[:, ts]
        srt = srt_ref[:, ts]
        r1 = rk3[:, 0:half]
        r2 = rk3[:, half:]
        rk_rot = jnp.concatenate([r1 * crt - r2 * srt, r1 * srt + r2 * crt], axis=1)
        rkt_ref[:, ts] = (rk_rot * (RET_HEAD_DIM ** -0.5)).reshape(RET_WIDTH, tm).astype(BF16)

        v_ref[:, :, ts] = feat_dot(*FEAT_DV).reshape(DIFF_HEADS, DV_DIM, tm).astype(BF16)
        rv_ref[ts, :] = tok_dot(*TOK_RV).astype(BF16)
        rg_ref[ts, :] = tok_dot(*TOK_RG)


def _inproj(x, ln1, wf, wt, qw_b, kw_b, tabs, layer, batch, seq):
    t = batch * seq
    tm = min(INPROJ_TILE, seq)
    nst = seq // tm
    cosd, sind, cosrt, sinrt, cosr, sinr = tabs
    lay3 = lambda i: (layer, 0, 0)
    in_specs = [
        pl.BlockSpec((tm, D_MODEL), lambda i: (i, 0)),
        pl.BlockSpec((None, 1, D_MODEL), lay3),
        pl.BlockSpec((None, FEAT_ROWS, D_MODEL), lay3),
        pl.BlockSpec((None, D_MODEL, TOK_COLS), lay3),
        pl.BlockSpec((None, QK_DIM, min(INPROJ_SUB, tm)), lay3),
        pl.BlockSpec((None, QK_DIM, min(INPROJ_SUB, tm)), lay3),
        pl.BlockSpec((8, tm), lambda i: (0, i % nst)),
        pl.BlockSpec((8, tm), lambda i: (0, i % nst)),
        pl.BlockSpec((32, tm), lambda i: (0, i % nst)),
        pl.BlockSpec((32, tm), lambda i: (0, i % nst)),
        pl.BlockSpec((tm, LANES), lambda i: (i % nst, 0)),
        pl.BlockSpec((tm, LANES), lambda i: (i % nst, 0)),
    ]
    feat_spec = pl.BlockSpec((None, 512, tm), lambda i: (i // nst, 0, i % nst))
    tok_spec = pl.BlockSpec((tm, 512), lambda i: (i, 0))
    out_specs = [
        feat_spec,
        tok_spec,
        pl.BlockSpec((None, DIFF_HEADS, DV_DIM, tm), lambda i: (i // nst, 0, 0, i % nst)),
        tok_spec,
        feat_spec,
        tok_spec,
        tok_spec,
    ]
    out_shape = [
        jax.ShapeDtypeStruct((batch, 512, seq), BF16),
        jax.ShapeDtypeStruct((t, 512), BF16),
        jax.ShapeDtypeStruct((batch, DIFF_HEADS, DV_DIM, seq), BF16),
        jax.ShapeDtypeStruct((t, 512), BF16),
        jax.ShapeDtypeStruct((batch, 512, seq), BF16),
        jax.ShapeDtypeStruct((t, 512), BF16),
        jax.ShapeDtypeStruct((t, 512), F32),
    ]
    return pl.pallas_call(
        _inproj_kernel, grid=(t // tm,), in_specs=in_specs, out_specs=out_specs, out_shape=out_shape,
        compiler_params=_cparams(1), name="inproj",
    )(x, ln1, wf, wt, qw_b, kw_b, cosd, sind, cosrt, sinrt, cosr, sinr)


def _attn_kernel(lp_ref, qw_ref, kw_ref, qt_ref, k_ref, v_ref, sw_ref, o_ref,
                 acc1, acc2, m1, m2, l1, l2, p_buf, *, lam_init, tk):
    seq = k_ref.shape[0]
    qt = qt_ref[...]
    row = lax.broadcasted_iota(jnp.int32, qt.shape, 0)
    zero = jnp.zeros_like(qt)
    qa = jnp.where(row < QK_DIM, qt, zero)
    qb = jnp.where(row >= QK_DIM, qt, zero)
    lp = lp_ref[...]
    lam = (jnp.exp(jnp.sum(lp[0:1] * lp[1:2], axis=1, keepdims=True))
           - jnp.exp(jnp.sum(lp[2:3] * lp[3:4], axis=1, keepdims=True)) + lam_init)

    def finish(out):
        oms = jnp.mean(out * out, axis=0, keepdims=True)
        y = out * lax.rsqrt(oms + EPS) * sw_ref[...]
        o_ref[...] = y.T.astype(BF16)

    bound = (jnp.max(jnp.abs(qw_ref[...]), axis=1, keepdims=True)
             * jnp.max(jnp.abs(kw_ref[...]), axis=1, keepdims=True) * (QK_DIM * BOUND_SLACK))
    kb = k_ref[...]
    row_sums = []
    for c, qx in enumerate((qa, qb)):
        e = jnp.exp2(jnp.dot(kb, qx, preferred_element_type=F32) - bound)
        row_sums.append(jnp.sum(e, axis=0, keepdims=True))
        p_buf[c] = e.astype(BF16)
    s1, s2 = row_sums
    ratio = (lam * s1 / s2).astype(BF16)
    w = p_buf[0] - ratio * p_buf[1]
    finish(jnp.dot(v_ref[...], w, preferred_element_type=F32) / s1)
    l_min = jnp.minimum(jnp.min(s1), jnp.min(s2))

    @pl.when(l_min < UNDERFLOW_GUARD)
    def _running_max_pass():
        comps = ((qa, acc1, m1, l1), (qb, acc2, m2, l2))
        for _, acc, m, l in comps:
            acc[...] = jnp.zeros_like(acc)
            l[...] = jnp.zeros_like(l)
            m[...] = jnp.full_like(m, NEG_BIG)

        def body(j, carry):
            off = pl.multiple_of(j * tk, tk)
            kblk = k_ref[pl.ds(off, tk), :]
            vblk = v_ref[:, pl.ds(off, tk)]
            for qx, acc, m, l in comps:
                s = jnp.dot(kblk, qx, preferred_element_type=F32)
                m_old = m[...]
                m_new = jnp.maximum(m_old, jnp.max(s, axis=0, keepdims=True))
                e = jnp.exp2(s - m_new)
                alpha = jnp.exp2(m_old - m_new)
                l[...] = l[...] * alpha + jnp.sum(e, axis=0, keepdims=True)
                acc[...] = acc[...] * alpha + jnp.dot(vblk, e.astype(BF16), preferred_element_type=F32)
                m[...] = m_new
            return carry

        lax.fori_loop(0, seq // tk, body, 0)
        finish(acc1[...] / l1[...] - lam * (acc2[...] / l2[...]))


def _diff_attention(lp, qw, kw, qt, k, v, sw_b, layer, lam_init, batch, seq):
    tq = min(Q_TILE, seq, ATTN_SCORE_ELEMS // seq)
    tk = min(KV_TILE, seq)
    k3 = k.reshape(batch, seq, DIFF_WIDTH)
    kern = functools.partial(_attn_kernel, lam_init=lam_init, tk=tk)
    lay3 = lambda b, h, i: (layer, 0, 0)
    return pl.pallas_call(
        kern,
        grid=(batch, DIFF_HEADS, seq // tq),
        in_specs=[
            pl.BlockSpec((None, 4, QK_DIM), lay3),
            pl.BlockSpec((None, 1, QK_DIM), lay3),
            pl.BlockSpec((None, 1, QK_DIM), lay3),
            pl.BlockSpec((None, DV_DIM, tq), lambda b, h, i: (b, h, i)),
            pl.BlockSpec((None, seq, DV_DIM), lambda b, h, i: (b, 0, h)),
            pl.BlockSpec((None, None, DV_DIM, seq), lambda b, h, i: (b, h, 0, 0)),
            pl.BlockSpec((None, DV_DIM, tq), lay3),
        ],
        out_specs=pl.BlockSpec((None, tq, DV_DIM), lambda b, h, i: (b, i, h)),
        out_shape=jax.ShapeDtypeStruct((batch, seq, DIFF_WIDTH), BF16),
        scratch_shapes=[pltpu.VMEM((DV_DIM, tq), F32), pltpu.VMEM((DV_DIM, tq), F32),
                        pltpu.VMEM((1, tq), F32), pltpu.VMEM((1, tq), F32),
                        pltpu.VMEM((1, tq), F32), pltpu.VMEM((1, tq), F32),
                        pltpu.VMEM((2, seq, tq), BF16)],
        compiler_params=_cparams(3), name="diff_attention",
    )(lp, qw, kw, qt, k3, v, sw_b)


def _ret_kernel(lgl_ref, lgr_ref, lgs_ref, gnw_ref, q_ref, kt_ref, v_ref, g_ref, o_ref,
                qdf, qdb, kdf, kdb, cdf, cdb, dcomb, gfull, s_st, t_st, t_all, o_buf, *, group):
    b = pl.program_id(0)
    ph = pl.program_id(1)
    j = pl.program_id(2)
    n_groups = pl.num_programs(2)
    c = CHUNK

    @pl.when((b == 0) & (ph == 0) & (j == 0))
    def _tables():
        lgl = jax.nn.log_sigmoid(lgl_ref[...])
        lgr = jax.nn.log_sigmoid(lgr_ref[...])
        lgs = jax.nn.log_sigmoid(lgs_ref[...])
        pos_row = lax.broadcasted_iota(jnp.int32, (c, RET_WIDTH), 0).astype(F32)
        qdf[...] = jnp.exp((pos_row + 1.0) * lgl[0, 0:1, :])
        qdb[...] = jnp.exp((c - pos_row) * lgl[1, 0:1, :])
        pos_lane = lax.broadcasted_iota(jnp.int32, (RET_WIDTH, c), 1).astype(F32)
        kdf[...] = jnp.exp((c - 1.0 - pos_lane) * lgr[0])
        kdb[...] = jnp.exp(pos_lane * lgr[1])
        cdf[...] = jnp.exp(float(c) * lgr[0])
        cdb[...] = jnp.exp(float(c) * lgr[1])
        qi = (lax.broadcasted_iota(jnp.int32, (RET_HEADS * c, c), 0) % c).astype(F32)
        ki = lax.broadcasted_iota(jnp.int32, (RET_HEADS * c, c), 1).astype(F32)
        rel = qi - ki
        dcomb[...] = jnp.where(rel >= 0, jnp.exp(rel * lgs[0]), jnp.exp(-rel * lgs[1]))

        frow = lax.broadcasted_iota(jnp.int32, (RET_WIDTH, RET_WIDTH), 0) // RET_HEAD_DIM
        fcol = lax.broadcasted_iota(jnp.int32, (RET_WIDTH, RET_WIDTH), 1) // RET_HEAD_DIM
        gfull[...] = jnp.where(frow == fcol, 1.0 / RET_HEAD_DIM, 0.0).astype(BF16)

    lane = lax.broadcasted_iota(jnp.int32, (c, c), 1)
    row = lax.broadcasted_iota(jnp.int32, (c, c), 0)
    lo = lane < RET_HEAD_DIM
    same_head = (row < RET_HEAD_DIM) == lo

    @pl.when((ph == 0) & (j == 0))
    def _init():
        t_st[...] = jnp.zeros_like(t_st)
        s_st[...] = jnp.zeros_like(s_st)

    @pl.when(ph == 0)
    def _backward_states():
        first = (n_groups - 1 - j) * group
        for p in range(RET_PAIRS):
            sl = slice(p * c, (p + 1) * c)
            t = t_st[p]
            for g in reversed(range(group)):
                gs = slice(g * c, (g + 1) * c)
                t_all[first + g, p] = t.astype(BF16)
                ktd = (kt_ref[sl, gs].astype(F32) * kdb[sl, :]).astype(BF16)
                kv = jnp.dot(ktd, v_ref[gs, sl], preferred_element_type=F32)
                t = t * cdb[sl, :] + jnp.where(same_head, kv, 0.0)
            t_st[p] = t

    @pl.when(ph == 1)
    def _forward():
        first = j * group
        combos = [(p, g) for p in range(RET_PAIRS) for g in range(group)]
        zq = jnp.zeros((c, c), BF16)

        def blk(p, g):
            return slice(g * c, (g + 1) * c), slice(p * c, (p + 1) * c)

        kvs, sds = {}, {}
        for p, g in combos:
            gs, sl = blk(p, g)
            ktp = kt_ref[sl, gs]
            ktd = (ktp.astype(F32) * kdf[sl, :]).astype(BF16)
            kvs[p, g] = jnp.dot(ktd, v_ref[gs, sl], preferred_element_type=F32)
            qp = q_ref[gs, sl]
            q2 = jnp.concatenate([jnp.where(lo, qp, zq), jnp.where(lo, zq, qp)], axis=0)
            s = jnp.dot(q2, ktp, preferred_element_type=F32)
            sds[p, g] = (s * dcomb[2 * p * c:2 * (p + 1) * c, :]).astype(BF16)
        states = {}
        for p in range(RET_PAIRS):
            sl = slice(p * c, (p + 1) * c)
            s_state = s_st[p]
            for g in range(group):
                states[p, g] = s_state.astype(BF16)
                s_state = s_state * cdf[sl, :] + jnp.where(same_head, kvs[p, g], 0.0)
            s_st[p] = s_state
        for p, g in combos:
            gs, sl = blk(p, g)
            qp = q_ref[gs, sl]
            vp = v_ref[gs, sl]
            qf = qp.astype(F32)
            sd = sds[p, g]
            lhs = jnp.concatenate([sd[0:c], sd[c:2 * c], (qf * qdf[:, sl]).astype(BF16),
                                   (qf * qdb[:, sl]).astype(BF16)], axis=1)
            rhs = jnp.concatenate([jnp.where(lo, vp, zq), jnp.where(lo, zq, vp),
                                   states[p, g], t_all[first + g, p]], axis=0)
            o_buf[gs, sl] = jnp.dot(lhs, rhs, preferred_element_type=F32)
        o = o_buf[...]
        gms = jnp.dot((o * o).astype(BF16), gfull[...], preferred_element_type=F32)
        r = o * lax.rsqrt(gms + EPS) * gnw_ref[0:1, :]
        gate = g_ref[...]
        o_ref[...] = (gate * jax.nn.sigmoid(gate) * r).astype(BF16)


def _retention(lg_lane, lg_row, lg_sc, gnw_b, rq, rkt, rv, rg, layer, batch, seq):
    c = CHUNK
    n = seq // c
    group = min(RET_GROUP, n)
    ng = n // group
    tg = group * c

    def tok_map(b, ph, j):
        return (b * ng + jnp.where(ph == 0, ng - 1 - j, j), 0)

    def feat_map(b, ph, j):
        return (b, 0, jnp.where(ph == 0, ng - 1 - j, j))

    def fwd_only_map(b, ph, j):
        return (b * ng + jnp.where(ph == 0, 0, j), 0)

    return pl.pallas_call(
        functools.partial(_ret_kernel, group=group),
        grid=(batch, 2, ng),
        in_specs=[
            pl.BlockSpec((None, 2, 8, RET_WIDTH), lambda b, ph, j: (layer, 0, 0, 0)),
            pl.BlockSpec((None, 2, RET_WIDTH, c), lambda b, ph, j: (layer, 0, 0, 0)),
            pl.BlockSpec((None, 2, RET_HEADS * c, c), lambda b, ph, j: (layer, 0, 0, 0)),
            pl.BlockSpec((None, 8, RET_WIDTH), lambda b, ph, j: (layer, 0, 0)),
            pl.BlockSpec((tg, RET_WIDTH), fwd_only_map),
            pl.BlockSpec((None, RET_WIDTH, tg), feat_map),
            pl.BlockSpec((tg, RET_WIDTH), tok_map),
            pl.BlockSpec((tg, RET_WIDTH), fwd_only_map),
        ],
        out_specs=pl.BlockSpec((tg, RET_WIDTH), fwd_only_map),
        out_shape=jax.ShapeDtypeStruct((batch * seq, RET_WIDTH), BF16),
        scratch_shapes=[
            pltpu.VMEM((c, RET_WIDTH), F32), pltpu.VMEM((c, RET_WIDTH), F32),
            pltpu.VMEM((RET_WIDTH, c), F32), pltpu.VMEM((RET_WIDTH, c), F32),
            pltpu.VMEM((RET_WIDTH, c), F32), pltpu.VMEM((RET_WIDTH, c), F32),
            pltpu.VMEM((RET_HEADS * c, c), F32),
            pltpu.VMEM((RET_WIDTH, RET_WIDTH), BF16),
            pltpu.VMEM((RET_PAIRS, c, c), F32), pltpu.VMEM((RET_PAIRS, c, c), F32),
            pltpu.VMEM((n, RET_PAIRS, c, c), BF16),
            pltpu.VMEM((tg, RET_WIDTH), F32),
        ],
        compiler_params=_cparams(3), name="retention",
    )(lg_lane, lg_row, lg_sc, gnw_b, rq, rkt, rv, rg)


def _post_kernel(x_ref, a_ref, r_ref, p_ref, wo_ref, ln2_ref, w1_ref, w2_ref, wg_ref, wp_ref, o_ref):
    sub = min(POST_SUB, x_ref.shape[0])
    for part in range(x_ref.shape[0] // sub):
        ts = slice(part * sub, (part + 1) * sub)
        x = x_ref[ts, :]
        x = x + jnp.dot(a_ref[ts, :], wo_ref[0:DIFF_WIDTH, :], preferred_element_type=F32)
        x = x + jnp.dot(r_ref[ts, :], wo_ref[DIFF_WIDTH:, :], preferred_element_type=F32)
        ms = jnp.mean(x * x, axis=-1, keepdims=True)
        h2 = (x * lax.rsqrt(ms + EPS) * ln2_ref[...]).astype(BF16)
        hid = jnp.dot(h2, w1_ref[...], preferred_element_type=F32)
        act = jnp.square(jnp.maximum(hid, 0.0)).astype(BF16)
        x = x + jnp.dot(act, w2_ref[...], preferred_element_type=F32)
        gate = jax.nn.sigmoid(jnp.dot(x.astype(BF16), wg_ref[...], preferred_element_type=F32))
        pe = jnp.dot(p_ref[ts, :].astype(BF16), wp_ref[...], preferred_element_type=F32)
        o_ref[ts, :] = x + gate * pe


def _post(x, a, r, p, wo, ln2, w1, w2, wg, wp, layer, batch, seq):
    t = batch * seq
    tm = min(TOKEN_TILE, seq)
    lay3 = lambda i: (layer, 0, 0)

    def wspec(shape):
        return pl.BlockSpec((None,) + shape, lay3, pipeline_mode=pl.Buffered(1))

    return pl.pallas_call(
        _post_kernel,
        grid=(t // tm,),
        in_specs=[
            pl.BlockSpec((tm, D_MODEL), lambda i: (i, 0)),
            pl.BlockSpec((tm, DIFF_WIDTH), lambda i: (i, 0)),
            pl.BlockSpec((tm, RET_WIDTH), lambda i: (i, 0)),
            pl.BlockSpec((None, tm, PLE_DIM), lambda i: (layer, i, 0)),
            wspec((D_MODEL, D_MODEL)),
            pl.BlockSpec((None, 1, D_MODEL), lay3),
            wspec((D_MODEL, D_FF)),
            wspec((D_FF, D_MODEL)),
            wspec((D_MODEL, D_MODEL)),
            wspec((PLE_DIM, D_MODEL)),
        ],
        out_specs=pl.BlockSpec((tm, D_MODEL), lambda i: (i, 0)),
        out_shape=jax.ShapeDtypeStruct((t, D_MODEL), F32),
        compiler_params=_cparams(1), name="post",
    )(x, a, r, p, wo, ln2, w1, w2, wg, wp)


def _prepare_params(ln1_w, w_in, diff_q_norm, diff_k_norm, diff_subln, ret_decay_logit, ret_gn,
                    w_out, ln2_w, w_mlp1, w_mlp2, w_ple_gate, w_ple_proj):
    depth = w_in.shape[0]
    tm = INPROJ_SUB
    dq, dk, dv, rq, rk, rv, rg = (w_in[:, :, i * 512:(i + 1) * 512] for i in range(7))
    wf = jnp.swapaxes(jnp.concatenate([dq, dk, dv, rk], axis=2), 1, 2).astype(BF16)
    wt = jnp.concatenate([rq, rv, rg], axis=2).astype(BF16)
    q_scale = (QK_DIM ** -0.5) * LOG2E
    qw_b = jnp.broadcast_to((diff_q_norm.astype(F32) * q_scale)[:, :, None], (depth, QK_DIM, tm))
    kw_b = jnp.broadcast_to(diff_k_norm.astype(F32)[:, :, None], (depth, QK_DIM, tm))
    lam_init = [0.8 - 0.6 * math.exp(-0.3 * i) for i in range(depth)]
    sub_scale = jnp.asarray([1.0 - li for li in lam_init], F32)[:, None]
    sw_b = jnp.broadcast_to((diff_subln.astype(F32) * sub_scale)[:, :, None], (depth, DV_DIM, Q_TILE))
    lgt = ret_decay_logit.astype(F32)
    lg_lane = jnp.broadcast_to(jnp.repeat(lgt, RET_HEAD_DIM, axis=2)[:, :, None, :], (depth, 2, 8, RET_WIDTH))
    lg_row = jnp.broadcast_to(jnp.repeat(lgt, RET_HEAD_DIM, axis=2)[:, :, :, None], (depth, 2, RET_WIDTH, CHUNK))
    lg_sc = jnp.broadcast_to(jnp.repeat(lgt, CHUNK, axis=2)[:, :, :, None], (depth, 2, RET_HEADS * CHUNK, CHUNK))
    gnw_b = jnp.broadcast_to(jnp.tile(ret_gn.astype(F32), (1, RET_HEADS))[:, None, :], (depth, 8, RET_WIDTH))
    return dict(
        ln1=ln1_w.astype(F32)[:, None, :], wf=wf, wt=wt, qw_b=qw_b, kw_b=kw_b, sw_b=sw_b, lam_init=lam_init,
        qw=(diff_q_norm.astype(F32) * q_scale)[:, None, :], kw=diff_k_norm.astype(F32)[:, None, :],
        lg_lane=lg_lane, lg_row=lg_row, lg_sc=lg_sc, gnw_b=gnw_b,
        wo=w_out.astype(BF16), ln2=ln2_w.astype(F32)[:, None, :], w1=w_mlp1.astype(BF16),
        w2=w_mlp2.astype(BF16), wg=w_ple_gate.astype(BF16), wp=w_ple_proj.astype(BF16))


def _run_trunk(x, p, prm, tabs, diff_lambda):
    batch, seq, _ = x.shape
    depth = p.shape[0]
    xf = x.reshape(batch * seq, D_MODEL)
    pf = p.reshape(depth, batch * seq, PLE_DIM)
    lp = diff_lambda.astype(F32)
    for i in range(depth):
        qt, k, v, rq, rkt, rv, rg = _inproj(xf, prm["ln1"], prm["wf"], prm["wt"], prm["qw_b"], prm["kw_b"],
                                            tabs, i, batch, seq)
        a = _diff_attention(lp, prm["qw"], prm["kw"], qt, k, v, prm["sw_b"], i, prm["lam_init"][i],
                            batch, seq)
        r = _retention(prm["lg_lane"], prm["lg_row"], prm["lg_sc"], prm["gnw_b"], rq, rkt, rv, rg,
                       i, batch, seq)
        xf = _post(xf, a.reshape(batch * seq, DIFF_WIDTH), r, pf, prm["wo"], prm["ln2"], prm["w1"],
                   prm["w2"], prm["wg"], prm["wp"], i, batch, seq)
    return xf.reshape(batch, seq, D_MODEL)


def kernel(x_prompt, x_sample, p_prompt, p_sample, ln1_w, w_in, diff_q_norm, diff_k_norm, diff_lambda,
           diff_subln, ret_decay_logit, ret_gn, w_out, ln2_w, w_mlp1, w_mlp2, w_ple_gate, w_ple_proj):
    prm = _prepare_params(ln1_w, w_in, diff_q_norm, diff_k_norm, diff_subln, ret_decay_logit, ret_gn,
                          w_out, ln2_w, w_mlp1, w_mlp2, w_ple_gate, w_ple_proj)
    tabs = _rope_tables(max(x_prompt.shape[1], x_sample.shape[1]))
    y_prompt = _run_trunk(x_prompt, p_prompt, prm, tabs, diff_lambda)
    y_sample = _run_trunk(x_sample, p_sample, prm, tabs, diff_lambda)
    return (y_prompt, y_sample)
```

```python
import functools
import math

import jax
import jax.numpy as jnp
from jax import lax
from jax.experimental import pallas as pl
from jax.experimental.pallas import tpu as pltpu

F32 = jnp.float32
BF16 = jnp.bfloat16

D_MODEL = 1024
DIFF_WIDTH = 512
RET_WIDTH = 512
QK_DIM = 64
DV_DIM = 128
DIFF_HEADS = 4
RET_HEAD_DIM = 64
RET_HEADS = 8
RET_PAIRS = RET_HEADS // 2
ROT_DIMS = 16
ROPE_THETA = 500000.0
RET_THETA = 10000.0
D_FF = 4096
PLE_DIM = 256
CHUNK = 128
EPS = 1e-6
LOG2E = 1.4426950408889634
NEG_BIG = -1e30

V7X_VMEM_BYTES = 64 * 1024 * 1024
VMEM_LIMIT_BYTES = V7X_VMEM_BYTES - 8 * 1024 * 1024
LANES = 128

FEAT_DQ, FEAT_DK, FEAT_DV, FEAT_RK = ((i * 512, (i + 1) * 512) for i in range(4))
TOK_RQ, TOK_RV, TOK_RG = ((i * 512, (i + 1) * 512) for i in range(3))
FEAT_ROWS = FEAT_RK[1]
TOK_COLS = TOK_RG[1]

TOKEN_TILE = 512
POST_SUB = 256
INPROJ_TILE = 1024
INPROJ_SUB = 256
Q_TILE = 1024
KV_TILE = 256
ATTN_SCORE_ELEMS = 4096 * 1024
BOUND_SLACK = 1.02
UNDERFLOW_GUARD = 2.0 ** -64
RET_GROUP = 8
TABLE_TILE = 1024


def _cparams(n_grid):
    return pltpu.CompilerParams(dimension_semantics=("arbitrary",) * n_grid,
                                vmem_limit_bytes=VMEM_LIMIT_BYTES)


def _const_spec(shape):
    nd = len(shape)
    return pl.BlockSpec(shape, lambda *_: (0,) * nd)


def _rope_kernel(invd_ref, invrt_ref, invr_ref, sgn_ref,
                 cosd_ref, sind_ref, cosrt_ref, sinrt_ref, cosr_ref, sinr_ref):
    ts = cosd_ref.shape[1]
    base = pl.program_id(0) * ts
    pos_d = (base + lax.broadcasted_iota(jnp.int32, (8, ts), 1)).astype(F32)
    ang_d = pos_d * invd_ref[...]
    cosd_ref[...] = jnp.cos(ang_d)
    sind_ref[...] = jnp.sin(ang_d)
    pos_r = (base + lax.broadcasted_iota(jnp.int32, (32, ts), 1)).astype(F32)
    ang_r = pos_r * invrt_ref[...]
    cosrt_ref[...] = jnp.cos(ang_r)
    sinrt_ref[...] = jnp.sin(ang_r)
    pos_t = (base + lax.broadcasted_iota(jnp.int32, (ts, LANES), 0)).astype(F32)
    ang_t = pos_t * invr_ref[0:1, :]
    cosr_ref[...] = jnp.cos(ang_t)
    sinr_ref[...] = jnp.sin(ang_t) * sgn_ref[0:1, :]


def _rope_tables(s_max):
    ts = min(TABLE_TILE, s_max)
    inv_d = 1.0 / (ROPE_THETA ** (jnp.arange(0, ROT_DIMS, 2, dtype=F32) / ROT_DIMS))
    inv_r = 1.0 / (RET_THETA ** (jnp.arange(0, RET_HEAD_DIM, 2, dtype=F32) / RET_HEAD_DIM))
    invd_b = jnp.broadcast_to(inv_d[:, None], (8, ts))
    invrt_b = jnp.broadcast_to(inv_r[:, None], (32, ts))
    invr_lane = jnp.broadcast_to(jnp.tile(inv_r, LANES // 32)[None, :], (8, LANES))
    sgn = jnp.where((jnp.arange(LANES) % RET_HEAD_DIM) < RET_HEAD_DIM // 2, -1.0, 1.0).astype(F32)
    sgn_b = jnp.broadcast_to(sgn[None, :], (8, LANES))
    return pl.pallas_call(
        _rope_kernel,
        grid=(s_max // ts,),
        in_specs=[_const_spec((8, ts)), _const_spec((32, ts)), _const_spec((8, LANES)),
                  _const_spec((8, LANES))],
        out_specs=[pl.BlockSpec((8, ts), lambda i: (0, i)), pl.BlockSpec((8, ts), lambda i: (0, i)),
                   pl.BlockSpec((32, ts), lambda i: (0, i)), pl.BlockSpec((32, ts), lambda i: (0, i)),
                   pl.BlockSpec((ts, LANES), lambda i: (i, 0)), pl.BlockSpec((ts, LANES), lambda i: (i, 0))],
        out_shape=[jax.ShapeDtypeStruct((8, s_max), F32), jax.ShapeDtypeStruct((8, s_max), F32),
                   jax.ShapeDtypeStruct((32, s_max), F32), jax.ShapeDtypeStruct((32, s_max), F32),
                   jax.ShapeDtypeStruct((s_max, LANES), F32), jax.ShapeDtypeStruct((s_max, LANES), F32)],
        compiler_params=_cparams(1),
        name="rope_tables",
    )(invd_b, invrt_b, invr_lane, sgn_b)


def _inproj_kernel(x_ref, ln_ref, wf_ref, wt_ref, qw_ref, kw_ref,
                   cd_ref, sd_ref, crt_ref, srt_ref, cr_ref, sr_ref,
                   qt_ref, k_ref, v_ref, rq_ref, rkt_ref, rv_ref, rg_ref):
    tm = min(INPROJ_SUB, x_ref.shape[0])
    half = RET_HEAD_DIM // 2

    def norm_rot(z, w, cd, sd):
        z3 = z.reshape(2 * DIFF_HEADS, QK_DIM, tm)
        gms = jnp.mean(z3 * z3, axis=1, keepdims=True)
        y = z3 * lax.rsqrt(gms + EPS) * w[None]
        y1 = y[:, 0:8]
        y2 = y[:, 8:16]
        out = jnp.concatenate([y1 * cd - y2 * sd, y1 * sd + y2 * cd, y[:, 16:]], axis=1)
        return out.reshape(DIFF_WIDTH, tm)

    for part in range(x_ref.shape[0] // tm):
        ts = slice(part * tm, (part + 1) * tm)
        x = x_ref[ts, :]
        ms = jnp.mean(x * x, axis=-1, keepdims=True)
        hn = (x * lax.rsqrt(ms + EPS) * ln_ref[...]).astype(BF16)

        def feat_dot(lo, hi):
            return lax.dot_general(wf_ref[lo:hi, :], hn, (((1,), (1,)), ((), ())),
                                   preferred_element_type=F32)

        def tok_dot(lo, hi):
            return jnp.dot(hn, wt_ref[:, lo:hi], preferred_element_type=F32)

        cd = cd_ref[:, ts]
        sd = sd_ref[:, ts]
        k_ref[ts, :] = norm_rot(feat_dot(*FEAT_DK), kw_ref[...], cd, sd).T.astype(BF16)
        qt_ref[:, ts] = norm_rot(feat_dot(*FEAT_DQ), qw_ref[...], cd, sd).astype(BF16)

        rq = tok_dot(*TOK_RQ)
        lane = lax.broadcasted_iota(jnp.int32, rq.shape, 1)
        partner = jnp.where((lane % RET_HEAD_DIM) < half,
                            pltpu.roll(rq, RET_WIDTH - half, 1), pltpu.roll(rq, half, 1))
        cr = jnp.concatenate([cr_ref[ts, :]] * (RET_WIDTH // LANES), axis=1)
        sr = jnp.concatenate([sr_ref[ts, :]] * (RET_WIDTH // LANES), axis=1)
        rq_ref[ts, :] = (rq * cr + partner * sr).astype(BF16)

        rk3 = feat_dot(*FEAT_RK).reshape(RET_HEADS, RET_HEAD_DIM, tm)
        crt = crt_ref[:, ts]
        srt = srt_ref[:, ts]
        r1 = rk3[:, 0:half]
        r2 = rk3[:, half:]
        rk_rot = jnp.concatenate([r1 * crt - r2 * srt, r1 * srt + r2 * crt], axis=1)
        rkt_ref[:, ts] = (rk_rot * (RET_HEAD_DIM ** -0.5)).reshape(RET_WIDTH, tm).astype(BF16)

        v_ref[:, :, ts] = feat_dot(*FEAT_DV).reshape(DIFF_HEADS, DV_DIM, tm).astype(BF16)
        rv_ref[ts, :] = tok_dot(*TOK_RV).astype(BF16)
        rg_ref[ts, :] = tok_dot(*TOK_RG)


def _inproj(x, ln1, wf, wt, qw_b, kw_b, tabs, layer, batch, seq):
    t = batch * seq
    tm = min(INPROJ_TILE, seq)
    nst = seq // tm
    cosd, sind, cosrt, sinrt, cosr, sinr = tabs
    lay3 = lambda i: (layer, 0, 0)
    in_specs = [
        pl.BlockSpec((tm, D_MODEL), lambda i: (i, 0)),
        pl.BlockSpec((None, 1, D_MODEL), lay3),
        pl.BlockSpec((None, FEAT_ROWS, D_MODEL), lay3),
        pl.BlockSpec((None, D_MODEL, TOK_COLS), lay3),
        pl.BlockSpec((None, QK_DIM, min(INPROJ_SUB, tm)), lay3),
        pl.BlockSpec((None, QK_DIM, min(INPROJ_SUB, tm)), lay3),
        pl.BlockSpec((8, tm), lambda i: (0, i % nst)),
        pl.BlockSpec((8, tm), lambda i: (0, i % nst)),
        pl.BlockSpec((32, tm), lambda i: (0, i % nst)),
        pl.BlockSpec((32, tm), lambda i: (0, i % nst)),
        pl.BlockSpec((tm, LANES), lambda i: (i % nst, 0)),
        pl.BlockSpec((tm, LANES), lambda i: (i % nst, 0)),
    ]
    feat_spec = pl.BlockSpec((None, 512, tm), lambda i: (i // nst, 0, i % nst))
    tok_spec = pl.BlockSpec((tm, 512), lambda i: (i, 0))
    out_specs = [
        feat_spec,
        tok_spec,
        pl.BlockSpec((None, DIFF_HEADS, DV_DIM, tm), lambda i: (i // nst, 0, 0, i % nst)),
        tok_spec,
        feat_spec,
        tok_spec,
        tok_spec,
    ]
    out_shape = [
        jax.ShapeDtypeStruct((batch, 512, seq), BF16),
        jax.ShapeDtypeStruct((t, 512), BF16),
        jax.ShapeDtypeStruct((batch, DIFF_HEADS, DV_DIM, seq), BF16),
        jax.ShapeDtypeStruct((t, 512), BF16),
        jax.ShapeDtypeStruct((batch, 512, seq), BF16),
        jax.ShapeDtypeStruct((t, 512), BF16),
        jax.ShapeDtypeStruct((t, 512), F32),
    ]
    return pl.pallas_call(
        _inproj_kernel, grid=(t // tm,), in_specs=in_specs, out_specs=out_specs, out_shape=out_shape,
        compiler_params=_cparams(1), name="inproj",
    )(x, ln1, wf, wt, qw_b, kw_b, cosd, sind, cosrt, sinrt, cosr, sinr)


def _attn_kernel(lp_ref, qw_ref, kw_ref, qt_ref, k_ref, v_ref, sw_ref, o_ref,
                 acc1, acc2, m1, m2, l1, l2, p_buf, *, lam_init, tk):
    seq = k_ref.shape[0]
    qt = qt_ref[...]
    row = lax.broadcasted_iota(jnp.int32, qt.shape, 0)
    zero = jnp.zeros_like(qt)
    qa = jnp.where(row < QK_DIM, qt, zero)
    qb = jnp.where(row >= QK_DIM, qt, zero)
    lp = lp_ref[...]
    lam = (jnp.exp(jnp.sum(lp[0:1] * lp[1:2], axis=1, keepdims=True))
           - jnp.exp(jnp.sum(lp[2:3] * lp[3:4], axis=1, keepdims=True)) + lam_init)

    def finish(out):
        oms = jnp.mean(out * out, axis=0, keepdims=True)
        y = out * lax.rsqrt(oms + EPS) * sw_ref[...]
        o_ref[...] = y.T.astype(BF16)

    bound = (jnp.max(jnp.abs(qw_ref[...]), axis=1, keepdims=True)
             * jnp.max(jnp.abs(kw_ref[...]), axis=1, keepdims=True) * (QK_DIM * BOUND_SLACK))
    kb = k_ref[...]
    row_sums = []
    for c, qx in enumerate((qa, qb)):
        e = jnp.exp2(jnp.dot(kb, qx, preferred_element_type=F32) - bound)
        row_sums.append(jnp.sum(e, axis=0, keepdims=True))
        p_buf[c] = e.astype(BF16)
    s1, s2 = row_sums
    ratio = (lam * s1 / s2).astype(BF16)
    w = p_buf[0] - ratio * p_buf[1]
    finish(jnp.dot(v_ref[...], w, preferred_element_type=F32) / s1)
    l_min = jnp.minimum(jnp.min(s1), jnp.min(s2))

    @pl.when(l_min < UNDERFLOW_GUARD)
    def _running_max_pass():
        comps = ((qa, acc1, m1, l1), (qb, acc2, m2, l2))
        for _, acc, m, l in comps:
            acc[...] = jnp.zeros_like(acc)
            l[...] = jnp.zeros_like(l)
            m[...] = jnp.full_like(m, NEG_BIG)

        def body(j, carry):
            off = pl.multiple_of(j * tk, tk)
            kblk = k_ref[pl.ds(off, tk), :]
            vblk = v_ref[:, pl.ds(off, tk)]
            for qx, acc, m, l in comps:
                s = jnp.dot(kblk, qx, preferred_element_type=F32)
                m_old = m[...]
                m_new = jnp.maximum(m_old, jnp.max(s, axis=0, keepdims=True))
                e = jnp.exp2(s - m_new)
                alpha = jnp.exp2(m_old - m_new)
                l[...] = l[...] * alpha + jnp.sum(e, axis=0, keepdims=True)
                acc[...] = acc[...] * alpha + jnp.dot(vblk, e.astype(BF16), preferred_element_type=F32)
                m[...] = m_new
            return carry

        lax.fori_loop(0, seq // tk, body, 0)
        finish(acc1[...] / l1[...] - lam * (acc2[...] / l2[...]))


def _diff_attention(lp, qw, kw, qt, k, v, sw_b, layer, lam_init, batch, seq):
    tq = min(Q_TILE, seq, ATTN_SCORE_ELEMS // seq)
    tk = min(KV_TILE, seq)
    k3 = k.reshape(batch, seq, DIFF_WIDTH)
    kern = functools.partial(_attn_kernel, lam_init=lam_init, tk=tk)
    lay3 = lambda b, h, i: (layer, 0, 0)
    return pl.pallas_call(
        kern,
        grid=(batch, DIFF_HEADS, seq // tq),
        in_specs=[
            pl.BlockSpec((None, 4, QK_DIM), lay3),
            pl.BlockSpec((None, 1, QK_DIM), lay3),
            pl.BlockSpec((None, 1, QK_DIM), lay3),
            pl.BlockSpec((None, DV_DIM, tq), lambda b, h, i: (b, h, i)),
            pl.BlockSpec((None, seq, DV_DIM), lambda b, h, i: (b, 0, h)),
            pl.BlockSpec((None, None, DV_DIM, seq), lambda b, h, i: (b, h, 0, 0)),
            pl.BlockSpec((None, DV_DIM, tq), lay3),
        ],
        out_specs=pl.BlockSpec((None, tq, DV_DIM), lambda b, h, i: (b, i, h)),
        out_shape=jax.ShapeDtypeStruct((batch, seq, DIFF_WIDTH), BF16),
        scratch_shapes=[pltpu.VMEM((DV_DIM, tq), F32), pltpu.VMEM((DV_DIM, tq), F32),
                        pltpu.VMEM((1, tq), F32), pltpu.VMEM((1, tq), F32),
                        pltpu.VMEM((1, tq), F32), pltpu.VMEM((1, tq), F32),
                        pltpu.VMEM((2, seq, tq), BF16)],
        compiler_params=_cparams(3), name="diff_attention",
    )(lp, qw, kw, qt, k3, v, sw_b)


def _ret_masks():
    c = CHUNK
    lane = lax.broadcasted_iota(jnp.int32, (c, c), 1)
    row = lax.broadcasted_iota(jnp.int32, (c, c), 0)
    lo = lane < RET_HEAD_DIM
    return lo, (row < RET_HEAD_DIM) == lo


def _ret_back_kernel(lgr_ref, kt_ref, v_ref, t_ref, kdb, cdb, t_st, *, group):
    b = pl.program_id(0)
    j = pl.program_id(1)
    c = CHUNK

    @pl.when((b == 0) & (j == 0))
    def _tables():
        lgb = jax.nn.log_sigmoid(lgr_ref[1])
        pos_lane = lax.broadcasted_iota(jnp.int32, (RET_WIDTH, c), 1).astype(F32)
        kdb[...] = jnp.exp(pos_lane * lgb)
        cdb[...] = jnp.exp(float(c) * lgb)

    @pl.when(j == 0)
    def _init():
        t_st[...] = jnp.zeros_like(t_st)

    _, same_head = _ret_masks()
    for p in range(RET_PAIRS):
        sl = slice(p * c, (p + 1) * c)
        t = t_st[p]
        for g in reversed(range(group)):
            gs = slice(g * c, (g + 1) * c)
            t_ref[g, p] = t.astype(BF16)
            ktd = (kt_ref[sl, gs].astype(F32) * kdb[sl, :]).astype(BF16)
            kv = jnp.dot(ktd, v_ref[gs, sl], preferred_element_type=F32)
            t = t * cdb[sl, :] + jnp.where(same_head, kv, 0.0)
        t_st[p] = t


def _ret_back(lg_row, rkt, rv, layer, batch, seq):
    c = CHUNK
    n = seq // c
    group = min(RET_GROUP, n)
    ng = n // group
    tg = group * c
    return pl.pallas_call(
        functools.partial(_ret_back_kernel, group=group),
        grid=(batch, ng),
        in_specs=[
            pl.BlockSpec((None, 2, RET_WIDTH, c), lambda b, j: (layer, 0, 0, 0)),
            pl.BlockSpec((None, RET_WIDTH, tg), lambda b, j: (b, 0, ng - 1 - j)),
            pl.BlockSpec((tg, RET_WIDTH), lambda b, j: (b * ng + ng - 1 - j, 0)),
        ],
        out_specs=pl.BlockSpec((None, group, RET_PAIRS, c, c), lambda b, j: (b, ng - 1 - j, 0, 0, 0)),
        out_shape=jax.ShapeDtypeStruct((batch, n, RET_PAIRS, c, c), BF16),
        scratch_shapes=[pltpu.VMEM((RET_WIDTH, c), F32), pltpu.VMEM((RET_WIDTH, c), F32),
                        pltpu.VMEM((RET_PAIRS, c, c), F32)],
        compiler_params=_cparams(2), name="retention_back",
    )(lg_row, rkt, rv)


def _ret_forward_tables(lgl_ref, lgr_ref, lgs_ref, qdf, qdb, kdf, cdf, dcomb, gfull):
    c = CHUNK
    lgl = jax.nn.log_sigmoid(lgl_ref[...])
    lgf = jax.nn.log_sigmoid(lgr_ref[0])
    lgs = jax.nn.log_sigmoid(lgs_ref[...])
    pos_row = lax.broadcasted_iota(jnp.int32, (c, RET_WIDTH), 0).astype(F32)
    qdf[...] = jnp.exp((pos_row + 1.0) * lgl[0, 0:1, :])
    qdb[...] = jnp.exp((c - pos_row) * lgl[1, 0:1, :])
    pos_lane = lax.broadcasted_iota(jnp.int32, (RET_WIDTH, c), 1).astype(F32)
    kdf[...] = jnp.exp((c - 1.0 - pos_lane) * lgf)
    cdf[...] = jnp.exp(float(c) * lgf)
    qi = (lax.broadcasted_iota(jnp.int32, (RET_HEADS * c, c), 0) % c).astype(F32)
    ki = lax.broadcasted_iota(jnp.int32, (RET_HEADS * c, c), 1).astype(F32)
    rel = qi - ki
    dcomb[...] = jnp.where(rel >= 0, jnp.exp(rel * lgs[0]), jnp.exp(-rel * lgs[1]))
    frow = lax.broadcasted_iota(jnp.int32, (RET_WIDTH, RET_WIDTH), 0) // RET_HEAD_DIM
    fcol = lax.broadcasted_iota(jnp.int32, (RET_WIDTH, RET_WIDTH), 1) // RET_HEAD_DIM
    gfull[...] = jnp.where(frow == fcol, 1.0 / RET_HEAD_DIM, 0.0).astype(BF16)


def _ret_forward(ts, q_ref, kt_ref, v_ref, g_ref, t_ref, gnw_ref, qdf, qdb, kdf, cdf, dcomb, gfull, s_st, o_buf):
    c = CHUNK
    chunks = range(ts.start // c, ts.stop // c)
    combos = [(p, g) for p in range(RET_PAIRS) for g in chunks]
    lo, same_head = _ret_masks()
    zq = jnp.zeros((c, c), BF16)

    def blk(p, g):
        return slice(g * c, (g + 1) * c), slice(p * c, (p + 1) * c)

    kvs, sds = {}, {}
    for p, g in combos:
        gs, sl = blk(p, g)
        ktp = kt_ref[sl, gs]
        ktd = (ktp.astype(F32) * kdf[sl, :]).astype(BF16)
        kvs[p, g] = jnp.dot(ktd, v_ref[gs, sl], preferred_element_type=F32)
        qp = q_ref[gs, sl]
        q2 = jnp.concatenate([jnp.where(lo, qp, zq), jnp.where(lo, zq, qp)], axis=0)
        s = jnp.dot(q2, ktp, preferred_element_type=F32)
        sds[p, g] = (s * dcomb[2 * p * c:2 * (p + 1) * c, :]).astype(BF16)
    states = {}
    for p in range(RET_PAIRS):
        sl = slice(p * c, (p + 1) * c)
        s_state = s_st[p]
        for g in chunks:
            states[p, g] = s_state.astype(BF16)
            s_state = s_state * cdf[sl, :] + jnp.where(same_head, kvs[p, g], 0.0)
        s_st[p] = s_state
    for p, g in combos:
        gs, sl = blk(p, g)
        qp = q_ref[gs, sl]
        vp = v_ref[gs, sl]
        qf = qp.astype(F32)
        sd = sds[p, g]
        lhs = jnp.concatenate([sd[0:c], sd[c:2 * c], (qf * qdf[:, sl]).astype(BF16),
                               (qf * qdb[:, sl]).astype(BF16)], axis=1)
        rhs = jnp.concatenate([jnp.where(lo, vp, zq), jnp.where(lo, zq, vp),
                               states[p, g], t_ref[g, p]], axis=0)
        o_buf[gs, sl] = jnp.dot(lhs, rhs, preferred_element_type=F32)
    o = o_buf[ts, :]
    gms = jnp.dot((o * o).astype(BF16), gfull[...], preferred_element_type=F32)
    r = o * lax.rsqrt(gms + EPS) * gnw_ref[0:1, :]
    gate = g_ref[ts, :]
    return (gate * jax.nn.sigmoid(gate) * r).astype(BF16)


def _post_kernel(x_ref, a_ref, p_ref, wo_ref, ln2_ref, w1_ref, w2_ref, wg_ref, wp_ref,
                 lgl_ref, lgr_ref, lgs_ref, gnw_ref, rq_ref, rkt_ref, rv_ref, rg_ref, t_ref, o_ref,
                 qdf, qdb, kdf, cdf, dcomb, gfull, s_st, o_buf, *, steps_per_seq):
    i = pl.program_id(0)

    @pl.when(i == 0)
    def _tables():
        _ret_forward_tables(lgl_ref, lgr_ref, lgs_ref, qdf, qdb, kdf, cdf, dcomb, gfull)

    @pl.when(i % steps_per_seq == 0)
    def _new_sequence():
        s_st[...] = jnp.zeros_like(s_st)

    sub = min(POST_SUB, x_ref.shape[0])
    for part in range(x_ref.shape[0] // sub):
        ts = slice(part * sub, (part + 1) * sub)
        r = _ret_forward(ts, rq_ref, rkt_ref, rv_ref, rg_ref, t_ref, gnw_ref,
                         qdf, qdb, kdf, cdf, dcomb, gfull, s_st, o_buf)
        x = x_ref[ts, :]
        x = x + jnp.dot(a_ref[ts, :], wo_ref[0:DIFF_WIDTH, :], preferred_element_type=F32)
        x = x + jnp.dot(r, wo_ref[DIFF_WIDTH:, :], preferred_element_type=F32)
        ms = jnp.mean(x * x, axis=-1, keepdims=True)
        h2 = (x * lax.rsqrt(ms + EPS) * ln2_ref[...]).astype(BF16)
        hid = jnp.dot(h2, w1_ref[...], preferred_element_type=F32)
        act = jnp.square(jnp.maximum(hid, 0.0)).astype(BF16)
        x = x + jnp.dot(act, w2_ref[...], preferred_element_type=F32)
        gate = jax.nn.sigmoid(jnp.dot(x.astype(BF16), wg_ref[...], preferred_element_type=F32))
        pe = jnp.dot(p_ref[ts, :].astype(BF16), wp_ref[...], preferred_element_type=F32)
        o_ref[ts, :] = x + gate * pe


def _post(x, a, p, wo, ln2, w1, w2, wg, wp, lg_lane, lg_row, lg_sc, gnw_b, rq, rkt, rv, rg, t_all,
          layer, batch, seq):
    t = batch * seq
    tm = min(TOKEN_TILE, seq)
    nst = seq // tm
    c = CHUNK
    lay3 = lambda i: (layer, 0, 0)
    lay4 = lambda i: (layer, 0, 0, 0)
    tok = lambda i: (i, 0)

    def wspec(shape):
        return pl.BlockSpec((None,) + shape, lay3, pipeline_mode=pl.Buffered(1))

    return pl.pallas_call(
        functools.partial(_post_kernel, steps_per_seq=nst),
        grid=(t // tm,),
        in_specs=[
            pl.BlockSpec((tm, D_MODEL), tok),
            pl.BlockSpec((tm, DIFF_WIDTH), tok),
            pl.BlockSpec((None, tm, PLE_DIM), lambda i: (layer, i, 0)),
            wspec((D_MODEL, D_MODEL)),
            pl.BlockSpec((None, 1, D_MODEL), lay3),
            wspec((D_MODEL, D_FF)),
            wspec((D_FF, D_MODEL)),
            wspec((D_MODEL, D_MODEL)),
            wspec((PLE_DIM, D_MODEL)),
            pl.BlockSpec((None, 2, 8, RET_WIDTH), lay4),
            pl.BlockSpec((None, 2, RET_WIDTH, c), lay4),
            pl.BlockSpec((None, 2, RET_HEADS * c, c), lay4),
            pl.BlockSpec((None, 8, RET_WIDTH), lay3),
            pl.BlockSpec((tm, RET_WIDTH), tok),
            pl.BlockSpec((None, RET_WIDTH, tm), lambda i: (i // nst, 0, i % nst)),
            pl.BlockSpec((tm, RET_WIDTH), tok),
            pl.BlockSpec((tm, RET_WIDTH), tok),
            pl.BlockSpec((None, tm // c, RET_PAIRS, c, c), lambda i: (i // nst, i % nst, 0, 0, 0)),
        ],
        out_specs=pl.BlockSpec((tm, D_MODEL), tok),
        out_shape=jax.ShapeDtypeStruct((t, D_MODEL), F32),
        scratch_shapes=[
            pltpu.VMEM((c, RET_WIDTH), F32), pltpu.VMEM((c, RET_WIDTH), F32),
            pltpu.VMEM((RET_WIDTH, c), F32), pltpu.VMEM((RET_WIDTH, c), F32),
            pltpu.VMEM((RET_HEADS * c, c), F32),
            pltpu.VMEM((RET_WIDTH, RET_WIDTH), BF16),
            pltpu.VMEM((RET_PAIRS, c, c), F32),
            pltpu.VMEM((tm, RET_WIDTH), F32),
        ],
        compiler_params=_cparams(1), name="post",
    )(x, a, p, wo, ln2, w1, w2, wg, wp, lg_lane, lg_row, lg_sc, gnw_b, rq, rkt, rv, rg, t_all)


def _prepare_params(ln1_w, w_in, diff_q_norm, diff_k_norm, diff_subln, ret_decay_logit, ret_gn,
                    w_out, ln2_w, w_mlp1, w_mlp2, w_ple_gate, w_ple_proj):
    depth = w_in.shape[0]
    tm = INPROJ_SUB
    dq, dk, dv, rq, rk, rv, rg = (w_in[:, :, i * 512:(i + 1) * 512] for i in range(7))
    wf = jnp.swapaxes(jnp.concatenate([dq, dk, dv, rk], axis=2), 1, 2).astype(BF16)
    wt = jnp.concatenate([rq, rv, rg], axis=2).astype(BF16)
    q_scale = (QK_DIM ** -0.5) * LOG2E
    qw_b = jnp.broadcast_to((diff_q_norm.astype(F32) * q_scale)[:, :, None], (depth, QK_DIM, tm))
    kw_b = jnp.broadcast_to(diff_k_norm.astype(F32)[:, :, None], (depth, QK_DIM, tm))
    lam_init = [0.8 - 0.6 * math.exp(-0.3 * i) for i in range(depth)]
    sub_scale = jnp.asarray([1.0 - li for li in lam_init], F32)[:, None]
    sw_b = jnp.broadcast_to((diff_subln.astype(F32) * sub_scale)[:, :, None], (depth, DV_DIM, Q_TILE))
    lgt = ret_decay_logit.astype(F32)
    lg_lane = jnp.broadcast_to(jnp.repeat(lgt, RET_HEAD_DIM, axis=2)[:, :, None, :], (depth, 2, 8, RET_WIDTH))
    lg_row = jnp.broadcast_to(jnp.repeat(lgt, RET_HEAD_DIM, axis=2)[:, :, :, None], (depth, 2, RET_WIDTH, CHUNK))
    lg_sc = jnp.broadcast_to(jnp.repeat(lgt, CHUNK, axis=2)[:, :, :, None], (depth, 2, RET_HEADS * CHUNK, CHUNK))
    gnw_b = jnp.broadcast_to(jnp.tile(ret_gn.astype(F32), (1, RET_HEADS))[:, None, :], (depth, 8, RET_WIDTH))
    return dict(
        ln1=ln1_w.astype(F32)[:, None, :], wf=wf, wt=wt, qw_b=qw_b, kw_b=kw_b, sw_b=sw_b, lam_init=lam_init,
        qw=(diff_q_norm.astype(F32) * q_scale)[:, None, :], kw=diff_k_norm.astype(F32)[:, None, :],
        lg_lane=lg_lane, lg_row=lg_row, lg_sc=lg_sc, gnw_b=gnw_b,
        wo=w_out.astype(BF16), ln2=ln2_w.astype(F32)[:, None, :], w1=w_mlp1.astype(BF16),
        w2=w_mlp2.astype(BF16), wg=w_ple_gate.astype(BF16), wp=w_ple_proj.astype(BF16))


def _run_trunk(x, p, prm, tabs, diff_lambda):
    batch, seq, _ = x.shape
    depth = p.shape[0]
    xf = x.reshape(batch * seq, D_MODEL)
    pf = p.reshape(depth, batch * seq, PLE_DIM)
    lp = diff_lambda.astype(F32)
    for i in range(depth):
        qt, k, v, rq, rkt, rv, rg = _inproj(xf, prm["ln1"], prm["wf"], prm["wt"], prm["qw_b"], prm["kw_b"],
                                            tabs, i, batch, seq)
        a = _diff_attention(lp, prm["qw"], prm["kw"], qt, k, v, prm["sw_b"], i, prm["lam_init"][i],
                            batch, seq)
        t_all = _ret_back(prm["lg_row"], rkt, rv, i, batch, seq)
        xf = _post(xf, a.reshape(batch * seq, DIFF_WIDTH), pf, prm["wo"], prm["ln2"], prm["w1"],
                   prm["w2"], prm["wg"], prm["wp"], prm["lg_lane"], prm["lg_row"], prm["lg_sc"],
                   prm["gnw_b"], rq, rkt, rv, rg, t_all, i, batch, seq)
    return xf.reshape(batch, seq, D_MODEL)


def kernel(x_prompt, x_sample, p_prompt, p_sample, ln1_w, w_in, diff_q_norm, diff_k_norm, diff_lambda,
           diff_subln, ret_decay_logit, ret_gn, w_out, ln2_w, w_mlp1, w_mlp2, w_ple_gate, w_ple_proj):
    prm = _prepare_params(ln1_w, w_in, diff_q_norm, diff_k_norm, diff_subln, ret_decay_logit, ret_gn,
                          w_out, ln2_w, w_mlp1, w_mlp2, w_ple_gate, w_ple_proj)
    tabs = _rope_tables(max(x_prompt.shape[1], x_sample.shape[1]))
    y_prompt = _run_trunk(x_prompt, p_prompt, prm, tabs, diff_lambda)
    y_sample = _run_trunk(x_sample, p_sample, prm, tabs, diff_lambda)
    return (y_prompt, y_sample)
```

```python
import functools
import math

import jax
import jax.numpy as jnp
from jax import lax
from jax.experimental import pallas as pl
from jax.experimental.pallas import tpu as pltpu

F32 = jnp.float32
BF16 = jnp.bfloat16

D_MODEL = 1024
DIFF_WIDTH = 512
RET_WIDTH = 512
QK_DIM = 64
DV_DIM = 128
DIFF_HEADS = 4
RET_HEAD_DIM = 64
RET_HEADS = 8
RET_PAIRS = RET_HEADS // 2
ROT_DIMS = 16
ROPE_THETA = 500000.0
RET_THETA = 10000.0
D_FF = 4096
PLE_DIM = 256
CHUNK = 128
EPS = 1e-6
LOG2E = 1.4426950408889634
NEG_BIG = -1e30

V7X_VMEM_BYTES = 64 * 1024 * 1024
VMEM_LIMIT_BYTES = V7X_VMEM_BYTES - 8 * 1024 * 1024
LANES = 128

FEAT_DQ, FEAT_DK, FEAT_DV, FEAT_RK = ((i * 512, (i + 1) * 512) for i in range(4))
TOK_RQ, TOK_RV, TOK_RG = ((i * 512, (i + 1) * 512) for i in range(3))
FEAT_ROWS = FEAT_RK[1]
TOK_COLS = TOK_RG[1]

TOKEN_TILE = 1024
POST_SUB = 512
INPROJ_TILE = 1024
INPROJ_SUB = 256
Q_TILE = 1024
KV_TILE = 256
ATTN_SCORE_ELEMS = 4096 * 1024
BOUND_SLACK = 1.02
UNDERFLOW_GUARD = 2.0 ** -64
RET_GROUP = 8
TABLE_TILE = 1024


def _cparams(n_grid):
    return pltpu.CompilerParams(dimension_semantics=("arbitrary",) * n_grid,
                                vmem_limit_bytes=VMEM_LIMIT_BYTES)


def _const_spec(shape):
    nd = len(shape)
    return pl.BlockSpec(shape, lambda *_: (0,) * nd)


def _rope_kernel(invd_ref, invrt_ref, invr_ref, sgn_ref,
                 cosd_ref, sind_ref, cosrt_ref, sinrt_ref, cosr_ref, sinr_ref):
    ts = cosd_ref.shape[1]
    base = pl.program_id(0) * ts
    pos_d = (base + lax.broadcasted_iota(jnp.int32, (8, ts), 1)).astype(F32)
    ang_d = pos_d * invd_ref[...]
    cosd_ref[...] = jnp.cos(ang_d)
    sind_ref[...] = jnp.sin(ang_d)
    pos_r = (base + lax.broadcasted_iota(jnp.int32, (32, ts), 1)).astype(F32)
    ang_r = pos_r * invrt_ref[...]
    cosrt_ref[...] = jnp.cos(ang_r)
    sinrt_ref[...] = jnp.sin(ang_r)
    pos_t = (base + lax.broadcasted_iota(jnp.int32, (ts, LANES), 0)).astype(F32)
    ang_t = pos_t * invr_ref[0:1, :]
    cosr_ref[...] = jnp.cos(ang_t)
    sinr_ref[...] = jnp.sin(ang_t) * sgn_ref[0:1, :]


def _rope_tables(s_max):
    ts = min(TABLE_TILE, s_max)
    inv_d = 1.0 / (ROPE_THETA ** (jnp.arange(0, ROT_DIMS, 2, dtype=F32) / ROT_DIMS))
    inv_r = 1.0 / (RET_THETA ** (jnp.arange(0, RET_HEAD_DIM, 2, dtype=F32) / RET_HEAD_DIM))
    invd_b = jnp.broadcast_to(inv_d[:, None], (8, ts))
    invrt_b = jnp.broadcast_to(inv_r[:, None], (32, ts))
    invr_lane = jnp.broadcast_to(jnp.tile(inv_r, LANES // 32)[None, :], (8, LANES))
    sgn = jnp.where((jnp.arange(LANES) % RET_HEAD_DIM) < RET_HEAD_DIM // 2, -1.0, 1.0).astype(F32)
    sgn_b = jnp.broadcast_to(sgn[None, :], (8, LANES))
    return pl.pallas_call(
        _rope_kernel,
        grid=(s_max // ts,),
        in_specs=[_const_spec((8, ts)), _const_spec((32, ts)), _const_spec((8, LANES)),
                  _const_spec((8, LANES))],
        out_specs=[pl.BlockSpec((8, ts), lambda i: (0, i)), pl.BlockSpec((8, ts), lambda i: (0, i)),
                   pl.BlockSpec((32, ts), lambda i: (0, i)), pl.BlockSpec((32, ts), lambda i: (0, i)),
                   pl.BlockSpec((ts, LANES), lambda i: (i, 0)), pl.BlockSpec((ts, LANES), lambda i: (i, 0))],
        out_shape=[jax.ShapeDtypeStruct((8, s_max), F32), jax.ShapeDtypeStruct((8, s_max), F32),
                   jax.ShapeDtypeStruct((32, s_max), F32), jax.ShapeDtypeStruct((32, s_max), F32),
                   jax.ShapeDtypeStruct((s_max, LANES), F32), jax.ShapeDtypeStruct((s_max, LANES), F32)],
        compiler_params=_cparams(1),
        name="rope_tables",
    )(invd_b, invrt_b, invr_lane, sgn_b)


def _inproj_kernel(x_ref, ln_ref, wf_ref, wt_ref, qw_ref, kw_ref,
                   cd_ref, sd_ref, crt_ref, srt_ref, cr_ref, sr_ref,
                   qt_ref, k_ref, v_ref, rq_ref, rkt_ref, rv_ref, rg_ref):
    tm = min(INPROJ_SUB, x_ref.shape[0])
    half = RET_HEAD_DIM // 2

    def norm_rot(z, w, cd, sd):
        z3 = z.reshape(2 * DIFF_HEADS, QK_DIM, tm)
        gms = jnp.mean(z3 * z3, axis=1, keepdims=True)
        y = z3 * lax.rsqrt(gms + EPS) * w[None]
        y1 = y[:, 0:8]
        y2 = y[:, 8:16]
        out = jnp.concatenate([y1 * cd - y2 * sd, y1 * sd + y2 * cd, y[:, 16:]], axis=1)
        return out.reshape(DIFF_WIDTH, tm)

    for part in range(x_ref.shape[0] // tm):
        ts = slice(part * tm, (part + 1) * tm)
        x = x_ref[ts, :]
        ms = jnp.mean(x * x, axis=-1, keepdims=True)
        hn = (x * lax.rsqrt(ms + EPS) * ln_ref[...]).astype(BF16)

        def feat_dot(lo, hi):
            return lax.dot_general(wf_ref[lo:hi, :], hn, (((1,), (1,)), ((), ())),
                                   preferred_element_type=F32)

        def tok_dot(lo, hi):
            return jnp.dot(hn, wt_ref[:, lo:hi], preferred_element_type=F32)

        cd = cd_ref[:, ts]
        sd = sd_ref[:, ts]
        k_ref[ts, :] = norm_rot(feat_dot(*FEAT_DK), kw_ref[...], cd, sd).T.astype(BF16)
        qt_ref[:, ts] = norm_rot(feat_dot(*FEAT_DQ), qw_ref[...], cd, sd).astype(BF16)

        rq = tok_dot(*TOK_RQ)
        lane = lax.broadcasted_iota(jnp.int32, rq.shape, 1)
        partner = jnp.where((lane % RET_HEAD_DIM) < half,
                            pltpu.roll(rq, RET_WIDTH - half, 1), pltpu.roll(rq, half, 1))
        cr = jnp.concatenate([cr_ref[ts, :]] * (RET_WIDTH // LANES), axis=1)
        sr = jnp.concatenate([sr_ref[ts, :]] * (RET_WIDTH // LANES), axis=1)
        rq_ref[ts, :] = (rq * cr + partner * sr).astype(BF16)

        rk3 = feat_dot(*FEAT_RK).reshape(RET_HEADS, RET_HEAD_DIM, tm)
        crt = crt_ref[:, ts]
        srt = srt_ref[:, ts]
        r1 = rk3[:, 0:half]
        r2 = rk3[:, half:]
        rk_rot = jnp.concatenate([r1 * crt - r2 * srt, r1 * srt + r2 * crt], axis=1)
        rkt_ref[:, ts] = (rk_rot * (RET_HEAD_DIM ** -0.5)).reshape(RET_WIDTH, tm).astype(BF16)

        v_ref[:, :, ts] = feat_dot(*FEAT_DV).reshape(DIFF_HEADS, DV_DIM, tm).astype(BF16)
        rv_ref[ts, :] = tok_dot(*TOK_RV).astype(BF16)
        rg_ref[ts, :] = tok_dot(*TOK_RG)


def _inproj(x, ln1, wf, wt, qw_b, kw_b, tabs, layer, batch, seq):
    t = batch * seq
    tm = min(INPROJ_TILE, seq)
    nst = seq // tm
    cosd, sind, cosrt, sinrt, cosr, sinr = tabs
    lay3 = lambda i: (layer, 0, 0)
    in_specs = [
        pl.BlockSpec((tm, D_MODEL), lambda i: (i, 0)),
        pl.BlockSpec((None, 1, D_MODEL), lay3),
        pl.BlockSpec((None, FEAT_ROWS, D_MODEL), lay3),
        pl.BlockSpec((None, D_MODEL, TOK_COLS), lay3),
        pl.BlockSpec((None, QK_DIM, min(INPROJ_SUB, tm)), lay3),
        pl.BlockSpec((None, QK_DIM, min(INPROJ_SUB, tm)), lay3),
        pl.BlockSpec((8, tm), lambda i: (0, i % nst)),
        pl.BlockSpec((8, tm), lambda i: (0, i % nst)),
        pl.BlockSpec((32, tm), lambda i: (0, i % nst)),
        pl.BlockSpec((32, tm), lambda i: (0, i % nst)),
        pl.BlockSpec((tm, LANES), lambda i: (i % nst, 0)),
        pl.BlockSpec((tm, LANES), lambda i: (i % nst, 0)),
    ]
    feat_spec = pl.BlockSpec((None, 512, tm), lambda i: (i // nst, 0, i % nst))
    tok_spec = pl.BlockSpec((tm, 512), lambda i: (i, 0))
    out_specs = [
        feat_spec,
        tok_spec,
        pl.BlockSpec((None, DIFF_HEADS, DV_DIM, tm), lambda i: (i // nst, 0, 0, i % nst)),
        tok_spec,
        feat_spec,
        tok_spec,
        tok_spec,
    ]
    out_shape = [
        jax.ShapeDtypeStruct((batch, 512, seq), BF16),
        jax.ShapeDtypeStruct((t, 512), BF16),
        jax.ShapeDtypeStruct((batch, DIFF_HEADS, DV_DIM, seq), BF16),
        jax.ShapeDtypeStruct((t, 512), BF16),
        jax.ShapeDtypeStruct((batch, 512, seq), BF16),
        jax.ShapeDtypeStruct((t, 512), BF16),
        jax.ShapeDtypeStruct((t, 512), F32),
    ]
    return pl.pallas_call(
        _inproj_kernel, grid=(t // tm,), in_specs=in_specs, out_specs=out_specs, out_shape=out_shape,
        compiler_params=_cparams(1), name="inproj",
    )(x, ln1, wf, wt, qw_b, kw_b, cosd, sind, cosrt, sinrt, cosr, sinr)


def _attn_kernel(lp_ref, qw_ref, kw_ref, qt_ref, k_ref, v_ref, sw_ref, o_ref,
                 acc1, acc2, m1, m2, l1, l2, p_buf, *, lam_init, tk):
    seq = k_ref.shape[0]
    qt = qt_ref[...]
    row = lax.broadcasted_iota(jnp.int32, qt.shape, 0)
    zero = jnp.zeros_like(qt)
    qa = jnp.where(row < QK_DIM, qt, zero)
    qb = jnp.where(row >= QK_DIM, qt, zero)
    lp = lp_ref[...]
    lam = (jnp.exp(jnp.sum(lp[0:1] * lp[1:2], axis=1, keepdims=True))
           - jnp.exp(jnp.sum(lp[2:3] * lp[3:4], axis=1, keepdims=True)) + lam_init)

    def finish(out):
        oms = jnp.mean(out * out, axis=0, keepdims=True)
        y = out * lax.rsqrt(oms + EPS) * sw_ref[...]
        o_ref[...] = y.astype(BF16)

    bound = (jnp.max(jnp.abs(qw_ref[...]), axis=1, keepdims=True)
             * jnp.max(jnp.abs(kw_ref[...]), axis=1, keepdims=True) * (QK_DIM * BOUND_SLACK))
    kb = k_ref[...]
    row_sums = []
    for c, qx in enumerate((qa, qb)):
        e = jnp.exp2(jnp.dot(kb, qx, preferred_element_type=F32) - bound)
        row_sums.append(jnp.sum(e, axis=0, keepdims=True))
        p_buf[c] = e.astype(BF16)
    s1, s2 = row_sums
    ratio = (lam * s1 / s2).astype(BF16)
    w = p_buf[0] - ratio * p_buf[1]
    finish(jnp.dot(v_ref[...], w, preferred_element_type=F32) / s1)
    l_min = jnp.minimum(jnp.min(s1), jnp.min(s2))

    @pl.when(l_min < UNDERFLOW_GUARD)
    def _running_max_pass():
        comps = ((qa, acc1, m1, l1), (qb, acc2, m2, l2))
        for _, acc, m, l in comps:
            acc[...] = jnp.zeros_like(acc)
            l[...] = jnp.zeros_like(l)
            m[...] = jnp.full_like(m, NEG_BIG)

        def body(j, carry):
            off = pl.multiple_of(j * tk, tk)
            kblk = k_ref[pl.ds(off, tk), :]
            vblk = v_ref[:, pl.ds(off, tk)]
            for qx, acc, m, l in comps:
                s = jnp.dot(kblk, qx, preferred_element_type=F32)
                m_old = m[...]
                m_new = jnp.maximum(m_old, jnp.max(s, axis=0, keepdims=True))
                e = jnp.exp2(s - m_new)
                alpha = jnp.exp2(m_old - m_new)
                l[...] = l[...] * alpha + jnp.sum(e, axis=0, keepdims=True)
                acc[...] = acc[...] * alpha + jnp.dot(vblk, e.astype(BF16), preferred_element_type=F32)
                m[...] = m_new
            return carry

        lax.fori_loop(0, seq // tk, body, 0)
        finish(acc1[...] / l1[...] - lam * (acc2[...] / l2[...]))


def _diff_attention(lp, qw, kw, qt, k, v, sw_b, layer, lam_init, batch, seq):
    tq = min(Q_TILE, seq, ATTN_SCORE_ELEMS // seq)
    tk = min(KV_TILE, seq)
    k3 = k.reshape(batch, seq, DIFF_WIDTH)
    kern = functools.partial(_attn_kernel, lam_init=lam_init, tk=tk)
    lay3 = lambda b, h, i: (layer, 0, 0)
    return pl.pallas_call(
        kern,
        grid=(batch, DIFF_HEADS, seq // tq),
        in_specs=[
            pl.BlockSpec((None, 4, QK_DIM), lay3),
            pl.BlockSpec((None, 1, QK_DIM), lay3),
            pl.BlockSpec((None, 1, QK_DIM), lay3),
            pl.BlockSpec((None, DV_DIM, tq), lambda b, h, i: (b, h, i)),
            pl.BlockSpec((None, seq, DV_DIM), lambda b, h, i: (b, 0, h)),
            pl.BlockSpec((None, None, DV_DIM, seq), lambda b, h, i: (b, h, 0, 0)),
            pl.BlockSpec((None, DV_DIM, tq), lay3),
        ],
        out_specs=pl.BlockSpec((None, DV_DIM, tq), lambda b, h, i: (b, h, i)),
        out_shape=jax.ShapeDtypeStruct((batch, DIFF_WIDTH, seq), BF16),
        scratch_shapes=[pltpu.VMEM((DV_DIM, tq), F32), pltpu.VMEM((DV_DIM, tq), F32),
                        pltpu.VMEM((1, tq), F32), pltpu.VMEM((1, tq), F32),
                        pltpu.VMEM((1, tq), F32), pltpu.VMEM((1, tq), F32),
                        pltpu.VMEM((2, seq, tq), BF16)],
        compiler_params=_cparams(3), name="diff_attention",
    )(lp, qw, kw, qt, k3, v, sw_b)


def _ret_kernel(lgl_ref, lgr_ref, lgs_ref, gnw_ref, q_ref, kt_ref, v_ref, g_ref, o_ref,
                qdf, qdb, kdf, kdb, cdf, cdb, dcomb, gfull, s_st, t_st, t_all, o_buf, *, group):
    b = pl.program_id(0)
    ph = pl.program_id(1)
    j = pl.program_id(2)
    n_groups = pl.num_programs(2)
    c = CHUNK

    @pl.when((b == 0) & (ph == 0) & (j == 0))
    def _tables():
        lgl = jax.nn.log_sigmoid(lgl_ref[...])
        lgr = jax.nn.log_sigmoid(lgr_ref[...])
        lgs = jax.nn.log_sigmoid(lgs_ref[...])
        pos_row = lax.broadcasted_iota(jnp.int32, (c, RET_WIDTH), 0).astype(F32)
        qdf[...] = jnp.exp((pos_row + 1.0) * lgl[0, 0:1, :])
        qdb[...] = jnp.exp((c - pos_row) * lgl[1, 0:1, :])
        pos_lane = lax.broadcasted_iota(jnp.int32, (RET_WIDTH, c), 1).astype(F32)
        kdf[...] = jnp.exp((c - 1.0 - pos_lane) * lgr[0])
        kdb[...] = jnp.exp(pos_lane * lgr[1])
        cdf[...] = jnp.exp(float(c) * lgr[0])
        cdb[...] = jnp.exp(float(c) * lgr[1])
        qi = (lax.broadcasted_iota(jnp.int32, (RET_HEADS * c, c), 0) % c).astype(F32)
        ki = lax.broadcasted_iota(jnp.int32, (RET_HEADS * c, c), 1).astype(F32)
        rel = qi - ki
        dcomb[...] = jnp.where(rel >= 0, jnp.exp(rel * lgs[0]), jnp.exp(-rel * lgs[1]))

        frow = lax.broadcasted_iota(jnp.int32, (RET_WIDTH, RET_WIDTH), 0) // RET_HEAD_DIM
        fcol = lax.broadcasted_iota(jnp.int32, (RET_WIDTH, RET_WIDTH), 1) // RET_HEAD_DIM
        gfull[...] = jnp.where(frow == fcol, 1.0 / RET_HEAD_DIM, 0.0).astype(BF16)

    lane = lax.broadcasted_iota(jnp.int32, (c, c), 1)
    row = lax.broadcasted_iota(jnp.int32, (c, c), 0)
    lo = lane < RET_HEAD_DIM
    same_head = (row < RET_HEAD_DIM) == lo

    @pl.when((ph == 0) & (j == 0))
    def _init():
        t_st[...] = jnp.zeros_like(t_st)
        s_st[...] = jnp.zeros_like(s_st)

    @pl.when(ph == 0)
    def _backward_states():
        first = (n_groups - 1 - j) * group
        for p in range(RET_PAIRS):
            sl = slice(p * c, (p + 1) * c)
            t = t_st[p]
            for g in reversed(range(group)):
                gs = slice(g * c, (g + 1) * c)
                t_all[first + g, p] = t.astype(BF16)
                ktd = (kt_ref[sl, gs].astype(F32) * kdb[sl, :]).astype(BF16)
                kv = jnp.dot(ktd, v_ref[gs, sl], preferred_element_type=F32)
                t = t * cdb[sl, :] + jnp.where(same_head, kv, 0.0)
            t_st[p] = t

    @pl.when(ph == 1)
    def _forward():
        first = j * group
        combos = [(p, g) for p in range(RET_PAIRS) for g in range(group)]
        zq = jnp.zeros((c, c), BF16)

        def blk(p, g):
            return slice(g * c, (g + 1) * c), slice(p * c, (p + 1) * c)

        kvs, sds = {}, {}
        for p, g in combos:
            gs, sl = blk(p, g)
            ktp = kt_ref[sl, gs]
            ktd = (ktp.astype(F32) * kdf[sl, :]).astype(BF16)
            kvs[p, g] = jnp.dot(ktd, v_ref[gs, sl], preferred_element_type=F32)
            qp = q_ref[gs, sl]
            q2 = jnp.concatenate([jnp.where(lo, qp, zq), jnp.where(lo, zq, qp)], axis=0)
            s = jnp.dot(q2, ktp, preferred_element_type=F32)
            sds[p, g] = (s * dcomb[2 * p * c:2 * (p + 1) * c, :]).astype(BF16)
        states = {}
        for p in range(RET_PAIRS):
            sl = slice(p * c, (p + 1) * c)
            s_state = s_st[p]
            for g in range(group):
                states[p, g] = s_state.astype(BF16)
                s_state = s_state * cdf[sl, :] + jnp.where(same_head, kvs[p, g], 0.0)
            s_st[p] = s_state
        for p, g in combos:
            gs, sl = blk(p, g)
            qp = q_ref[gs, sl]
            vp = v_ref[gs, sl]
            qf = qp.astype(F32)
            sd = sds[p, g]
            lhs = jnp.concatenate([sd[0:c], sd[c:2 * c], (qf * qdf[:, sl]).astype(BF16),
                                   (qf * qdb[:, sl]).astype(BF16)], axis=1)
            rhs = jnp.concatenate([jnp.where(lo, vp, zq), jnp.where(lo, zq, vp),
                                   states[p, g], t_all[first + g, p]], axis=0)
            o_buf[gs, sl] = jnp.dot(lhs, rhs, preferred_element_type=F32)
        o = o_buf[...]
        gms = jnp.dot((o * o).astype(BF16), gfull[...], preferred_element_type=F32)
        r = o * lax.rsqrt(gms + EPS) * gnw_ref[0:1, :]
        gate = g_ref[...]
        o_ref[...] = (gate * jax.nn.sigmoid(gate) * r).astype(BF16)


def _retention(lg_lane, lg_row, lg_sc, gnw_b, rq, rkt, rv, rg, layer, batch, seq):
    c = CHUNK
    n = seq // c
    group = min(RET_GROUP, n)
    ng = n // group
    tg = group * c

    def tok_map(b, ph, j):
        return (b * ng + jnp.where(ph == 0, ng - 1 - j, j), 0)

    def feat_map(b, ph, j):
        return (b, 0, jnp.where(ph == 0, ng - 1 - j, j))

    def fwd_only_map(b, ph, j):
        return (b * ng + jnp.where(ph == 0, 0, j), 0)

    return pl.pallas_call(
        functools.partial(_ret_kernel, group=group),
        grid=(batch, 2, ng),
        in_specs=[
            pl.BlockSpec((None, 2, 8, RET_WIDTH), lambda b, ph, j: (layer, 0, 0, 0)),
            pl.BlockSpec((None, 2, RET_WIDTH, c), lambda b, ph, j: (layer, 0, 0, 0)),
            pl.BlockSpec((None, 2, RET_HEADS * c, c), lambda b, ph, j: (layer, 0, 0, 0)),
            pl.BlockSpec((None, 8, RET_WIDTH), lambda b, ph, j: (layer, 0, 0)),
            pl.BlockSpec((tg, RET_WIDTH), fwd_only_map),
            pl.BlockSpec((None, RET_WIDTH, tg), feat_map),
            pl.BlockSpec((tg, RET_WIDTH), tok_map),
            pl.BlockSpec((tg, RET_WIDTH), fwd_only_map),
        ],
        out_specs=pl.BlockSpec((tg, RET_WIDTH), fwd_only_map),
        out_shape=jax.ShapeDtypeStruct((batch * seq, RET_WIDTH), BF16),
        scratch_shapes=[
            pltpu.VMEM((c, RET_WIDTH), F32), pltpu.VMEM((c, RET_WIDTH), F32),
            pltpu.VMEM((RET_WIDTH, c), F32), pltpu.VMEM((RET_WIDTH, c), F32),
            pltpu.VMEM((RET_WIDTH, c), F32), pltpu.VMEM((RET_WIDTH, c), F32),
            pltpu.VMEM((RET_HEADS * c, c), F32),
            pltpu.VMEM((RET_WIDTH, RET_WIDTH), BF16),
            pltpu.VMEM((RET_PAIRS, c, c), F32), pltpu.VMEM((RET_PAIRS, c, c), F32),
            pltpu.VMEM((n, RET_PAIRS, c, c), BF16),
            pltpu.VMEM((tg, RET_WIDTH), F32),
        ],
        compiler_params=_cparams(3), name="retention",
    )(lg_lane, lg_row, lg_sc, gnw_b, rq, rkt, rv, rg)


def _post_kernel(x_ref, a_ref, r_ref, p_ref, wo_ref, ln2_ref, w1_ref, w2_ref, wg_ref, wp_ref, o_ref):
    sub = min(POST_SUB, x_ref.shape[0])
    for part in range(x_ref.shape[0] // sub):
        ts = slice(part * sub, (part + 1) * sub)
        x = x_ref[ts, :]
        a_tok = a_ref[:, ts].astype(F32).T.astype(BF16)
        x = x + jnp.dot(a_tok, wo_ref[0:DIFF_WIDTH, :], preferred_element_type=F32)
        x = x + jnp.dot(r_ref[ts, :], wo_ref[DIFF_WIDTH:, :], preferred_element_type=F32)
        ms = jnp.mean(x * x, axis=-1, keepdims=True)
        h2 = (x * lax.rsqrt(ms + EPS) * ln2_ref[...]).astype(BF16)
        hid = jnp.dot(h2, w1_ref[...], preferred_element_type=F32)
        act = jnp.square(jnp.maximum(hid, 0.0)).astype(BF16)
        x = x + jnp.dot(act, w2_ref[...], preferred_element_type=F32)
        gate = jax.nn.sigmoid(jnp.dot(x.astype(BF16), wg_ref[...], preferred_element_type=F32))
        pe = jnp.dot(p_ref[ts, :].astype(BF16), wp_ref[...], preferred_element_type=F32)
        o_ref[ts, :] = x + gate * pe


def _post(x, a, r, p, wo, ln2, w1, w2, wg, wp, layer, batch, seq):
    t = batch * seq
    tm = min(TOKEN_TILE, seq)
    nst = seq // tm
    lay3 = lambda i: (layer, 0, 0)

    def wspec(shape):
        return pl.BlockSpec((None,) + shape, lay3, pipeline_mode=pl.Buffered(1))

    return pl.pallas_call(
        _post_kernel,
        grid=(t // tm,),
        in_specs=[
            pl.BlockSpec((tm, D_MODEL), lambda i: (i, 0)),
            pl.BlockSpec((None, DIFF_WIDTH, tm), lambda i: (i // nst, 0, i % nst)),
            pl.BlockSpec((tm, RET_WIDTH), lambda i: (i, 0)),
            pl.BlockSpec((None, tm, PLE_DIM), lambda i: (layer, i, 0)),
            wspec((D_MODEL, D_MODEL)),
            pl.BlockSpec((None, 1, D_MODEL), lay3),
            wspec((D_MODEL, D_FF)),
            wspec((D_FF, D_MODEL)),
            wspec((D_MODEL, D_MODEL)),
            wspec((PLE_DIM, D_MODEL)),
        ],
        out_specs=pl.BlockSpec((tm, D_MODEL), lambda i: (i, 0)),
        out_shape=jax.ShapeDtypeStruct((t, D_MODEL), F32),
        compiler_params=_cparams(1), name="post",
    )(x, a, r, p, wo, ln2, w1, w2, wg, wp)


def _prepare_params(ln1_w, w_in, diff_q_norm, diff_k_norm, diff_subln, ret_decay_logit, ret_gn,
                    w_out, ln2_w, w_mlp1, w_mlp2, w_ple_gate, w_ple_proj):
    depth = w_in.shape[0]
    tm = INPROJ_SUB
    dq, dk, dv, rq, rk, rv, rg = (w_in[:, :, i * 512:(i + 1) * 512] for i in range(7))
    wf = jnp.swapaxes(jnp.concatenate([dq, dk, dv, rk], axis=2), 1, 2).astype(BF16)
    wt = jnp.concatenate([rq, rv, rg], axis=2).astype(BF16)
    q_scale = (QK_DIM ** -0.5) * LOG2E
    qw_b = jnp.broadcast_to((diff_q_norm.astype(F32) * q_scale)[:, :, None], (depth, QK_DIM, tm))
    kw_b = jnp.broadcast_to(diff_k_norm.astype(F32)[:, :, None], (depth, QK_DIM, tm))
    lam_init = [0.8 - 0.6 * math.exp(-0.3 * i) for i in range(depth)]
    sub_scale = jnp.asarray([1.0 - li for li in lam_init], F32)[:, None]
    sw_b = jnp.broadcast_to((diff_subln.astype(F32) * sub_scale)[:, :, None], (depth, DV_DIM, Q_TILE))
    lgt = ret_decay_logit.astype(F32)
    lg_lane = jnp.broadcast_to(jnp.repeat(lgt, RET_HEAD_DIM, axis=2)[:, :, None, :], (depth, 2, 8, RET_WIDTH))
    lg_row = jnp.broadcast_to(jnp.repeat(lgt, RET_HEAD_DIM, axis=2)[:, :, :, None], (depth, 2, RET_WIDTH, CHUNK))
    lg_sc = jnp.broadcast_to(jnp.repeat(lgt, CHUNK, axis=2)[:, :, :, None], (depth, 2, RET_HEADS * CHUNK, CHUNK))
    gnw_b = jnp.broadcast_to(jnp.tile(ret_gn.astype(F32), (1, RET_HEADS))[:, None, :], (depth, 8, RET_WIDTH))
    return dict(
        ln1=ln1_w.astype(F32)[:, None, :], wf=wf, wt=wt, qw_b=qw_b, kw_b=kw_b, sw_b=sw_b, lam_init=lam_init,
        qw=(diff_q_norm.astype(F32) * q_scale)[:, None, :], kw=diff_k_norm.astype(F32)[:, None, :],
        lg_lane=lg_lane, lg_row=lg_row, lg_sc=lg_sc, gnw_b=gnw_b,
        wo=w_out.astype(BF16), ln2=ln2_w.astype(F32)[:, None, :], w1=w_mlp1.astype(BF16),
        w2=w_mlp2.astype(BF16), wg=w_ple_gate.astype(BF16), wp=w_ple_proj.astype(BF16))


def _run_trunk(x, p, prm, tabs, diff_lambda):
    batch, seq, _ = x.shape
    depth = p.shape[0]
    xf = x.reshape(batch * seq, D_MODEL)
    pf = p.reshape(depth, batch * seq, PLE_DIM)
    lp = diff_lambda.astype(F32)
    for i in range(depth):
        qt, k, v, rq, rkt, rv, rg = _inproj(xf, prm["ln1"], prm["wf"], prm["wt"], prm["qw_b"], prm["kw_b"],
                                            tabs, i, batch, seq)
        a = _diff_attention(lp, prm["qw"], prm["kw"], qt, k, v, prm["sw_b"], i, prm["lam_init"][i],
                            batch, seq)
        r = _retention(prm["lg_lane"], prm["lg_row"], prm["lg_sc"], prm["gnw_b"], rq, rkt, rv, rg,
                       i, batch, seq)
        xf = _post(xf, a, r, pf, prm["wo"], prm["ln2"], prm["w1"],
                   prm["w2"], prm["wg"], prm["wp"], i, batch, seq)
    return xf.reshape(batch, seq, D_MODEL)


def kernel(x_prompt, x_sample, p_prompt, p_sample, ln1_w, w_in, diff_q_norm, diff_k_norm, diff_lambda,
           diff_subln, ret_decay_logit, ret_gn, w_out, ln2_w, w_mlp1, w_mlp2, w_ple_gate, w_ple_proj):
    prm = _prepare_params(ln1_w, w_in, diff_q_norm, diff_k_norm, diff_subln, ret_decay_logit, ret_gn,
                          w_out, ln2_w, w_mlp1, w_mlp2, w_ple_gate, w_ple_proj)
    tabs = _rope_tables(max(x_prompt.shape[1], x_sample.shape[1]))
    y_prompt = _run_trunk(x_prompt, p_prompt, prm, tabs, diff_lambda)
    y_sample = _run_trunk(x_sample, p_sample, prm, tabs, diff_lambda)
    return (y_prompt, y_sample)
```

```python
import functools
import math

import jax
import jax.numpy as jnp
from jax import lax
from jax.experimental import pallas as pl
from jax.experimental.pallas import tpu as pltpu

F32 = jnp.float32
BF16 = jnp.bfloat16

D_MODEL = 1024
DIFF_WIDTH = 512
RET_WIDTH = 512
QK_DIM = 64
DV_DIM = 128
DIFF_HEADS = 4
RET_HEAD_DIM = 64
RET_HEADS = 8
RET_PAIRS = RET_HEADS // 2
ROT_DIMS = 16
ROPE_THETA = 500000.0
RET_THETA = 10000.0
D_FF = 4096
PLE_DIM = 256
CHUNK = 128
EPS = 1e-6
LOG2E = 1.4426950408889634
NEG_BIG = -1e30

V7X_VMEM_BYTES = 64 * 1024 * 1024
VMEM_LIMIT_BYTES = V7X_VMEM_BYTES - 8 * 1024 * 1024
LANES = 128

FEAT_DQ, FEAT_DK, FEAT_DV, FEAT_RK = ((i * 512, (i + 1) * 512) for i in range(4))
TOK_RQ, TOK_RV, TOK_RG = ((i * 512, (i + 1) * 512) for i in range(3))
FEAT_ROWS = FEAT_RK[1]
TOK_COLS = TOK_RG[1]

TOKEN_TILE = 1024
POST_SUB = 512
INPROJ_TILE = 1024
INPROJ_SUB = 256
Q_TILE = 1024
KV_TILE = 256
ATTN_SCORE_ELEMS = 4096 * 1024
BOUND_SLACK = 1.02
UNDERFLOW_GUARD = 2.0 ** -64
RET_GROUP = 16
TABLE_TILE = 1024


def _cparams(n_grid):
    return pltpu.CompilerParams(dimension_semantics=("arbitrary",) * n_grid,
                                vmem_limit_bytes=VMEM_LIMIT_BYTES)


def _const_spec(shape):
    nd = len(shape)
    return pl.BlockSpec(shape, lambda *_: (0,) * nd)


def _rope_kernel(invd_ref, invrt_ref, invr_ref, sgn_ref,
                 cosd_ref, sind_ref, cosrt_ref, sinrt_ref, cosr_ref, sinr_ref):
    ts = cosd_ref.shape[1]
    base = pl.program_id(0) * ts
    pos_d = (base + lax.broadcasted_iota(jnp.int32, (8, ts), 1)).astype(F32)
    ang_d = pos_d * invd_ref[...]
    cosd_ref[...] = jnp.cos(ang_d)
    sind_ref[...] = jnp.sin(ang_d)
    pos_r = (base + lax.broadcasted_iota(jnp.int32, (32, ts), 1)).astype(F32)
    ang_r = pos_r * invrt_ref[...]
    cosrt_ref[...] = jnp.cos(ang_r)
    sinrt_ref[...] = jnp.sin(ang_r)
    pos_t = (base + lax.broadcasted_iota(jnp.int32, (ts, LANES), 0)).astype(F32)
    ang_t = pos_t * invr_ref[0:1, :]
    cosr_ref[...] = jnp.cos(ang_t)
    sinr_ref[...] = jnp.sin(ang_t) * sgn_ref[0:1, :]


def _rope_tables(s_max):
    ts = min(TABLE_TILE, s_max)
    inv_d = 1.0 / (ROPE_THETA ** (jnp.arange(0, ROT_DIMS, 2, dtype=F32) / ROT_DIMS))
    inv_r = 1.0 / (RET_THETA ** (jnp.arange(0, RET_HEAD_DIM, 2, dtype=F32) / RET_HEAD_DIM))
    invd_b = jnp.broadcast_to(inv_d[:, None], (8, ts))
    invrt_b = jnp.broadcast_to(inv_r[:, None], (32, ts))
    invr_lane = jnp.broadcast_to(jnp.tile(inv_r, LANES // 32)[None, :], (8, LANES))
    sgn = jnp.where((jnp.arange(LANES) % RET_HEAD_DIM) < RET_HEAD_DIM // 2, -1.0, 1.0).astype(F32)
    sgn_b = jnp.broadcast_to(sgn[None, :], (8, LANES))
    return pl.pallas_call(
        _rope_kernel,
        grid=(s_max // ts,),
        in_specs=[_const_spec((8, ts)), _const_spec((32, ts)), _const_spec((8, LANES)),
                  _const_spec((8, LANES))],
        out_specs=[pl.BlockSpec((8, ts), lambda i: (0, i)), pl.BlockSpec((8, ts), lambda i: (0, i)),
                   pl.BlockSpec((32, ts), lambda i: (0, i)), pl.BlockSpec((32, ts), lambda i: (0, i)),
                   pl.BlockSpec((ts, LANES), lambda i: (i, 0)), pl.BlockSpec((ts, LANES), lambda i: (i, 0))],
        out_shape=[jax.ShapeDtypeStruct((8, s_max), F32), jax.ShapeDtypeStruct((8, s_max), F32),
                   jax.ShapeDtypeStruct((32, s_max), F32), jax.ShapeDtypeStruct((32, s_max), F32),
                   jax.ShapeDtypeStruct((s_max, LANES), F32), jax.ShapeDtypeStruct((s_max, LANES), F32)],
        compiler_params=_cparams(1),
        name="rope_tables",
    )(invd_b, invrt_b, invr_lane, sgn_b)


def _inproj_kernel(x_ref, ln_ref, wf_ref, wt_ref, qw_ref, kw_ref,
                   cd_ref, sd_ref, crt_ref, srt_ref, cr_ref, sr_ref,
                   qt_ref, k_ref, v_ref, rq_ref, rkt_ref, rv_ref, rg_ref):
    tm = min(INPROJ_SUB, x_ref.shape[0])
    half = RET_HEAD_DIM // 2

    def norm_rot(z, w, cd, sd):
        z3 = z.reshape(2 * DIFF_HEADS, QK_DIM, tm)
        gms = jnp.mean(z3 * z3, axis=1, keepdims=True)
        y = z3 * lax.rsqrt(gms + EPS) * w[None]
        y1 = y[:, 0:8]
        y2 = y[:, 8:16]
        out = jnp.concatenate([y1 * cd - y2 * sd, y1 * sd + y2 * cd, y[:, 16:]], axis=1)
        return out.reshape(DIFF_WIDTH, tm)

    for part in range(x_ref.shape[0] // tm):
        ts = slice(part * tm, (part + 1) * tm)
        x = x_ref[ts, :]
        ms = jnp.mean(x * x, axis=-1, keepdims=True)
        hn = (x * lax.rsqrt(ms + EPS) * ln_ref[...]).astype(BF16)

        def feat_dot(lo, hi):
            return lax.dot_general(wf_ref[lo:hi, :], hn, (((1,), (1,)), ((), ())),
                                   preferred_element_type=F32)

        def tok_dot(lo, hi):
            return jnp.dot(hn, wt_ref[:, lo:hi], preferred_element_type=F32)

        cd = cd_ref[:, ts]
        sd = sd_ref[:, ts]
        k_ref[ts, :] = norm_rot(feat_dot(*FEAT_DK), kw_ref[...], cd, sd).T.astype(BF16)
        qt_ref[:, ts] = norm_rot(feat_dot(*FEAT_DQ), qw_ref[...], cd, sd).astype(BF16)

        rq = tok_dot(*TOK_RQ)
        lane = lax.broadcasted_iota(jnp.int32, rq.shape, 1)
        partner = jnp.where((lane % RET_HEAD_DIM) < half,
                            pltpu.roll(rq, RET_WIDTH - half, 1), pltpu.roll(rq, half, 1))
        cr = jnp.concatenate([cr_ref[ts, :]] * (RET_WIDTH // LANES), axis=1)
        sr = jnp.concatenate([sr_ref[ts, :]] * (RET_WIDTH // LANES), axis=1)
        rq_ref[ts, :] = (rq * cr + partner * sr).astype(BF16)

        rk3 = feat_dot(*FEAT_RK).reshape(RET_HEADS, RET_HEAD_DIM, tm)
        crt = crt_ref[:, ts]
        srt = srt_ref[:, ts]
        r1 = rk3[:, 0:half]
        r2 = rk3[:, half:]
        rk_rot = jnp.concatenate([r1 * crt - r2 * srt, r1 * srt + r2 * crt], axis=1)
        rkt_ref[:, ts] = (rk_rot * (RET_HEAD_DIM ** -0.5)).reshape(RET_WIDTH, tm).astype(BF16)

        v_ref[:, :, ts] = feat_dot(*FEAT_DV).reshape(DIFF_HEADS, DV_DIM, tm).astype(BF16)
        rv_ref[ts, :] = tok_dot(*TOK_RV).astype(BF16)
        rg_ref[ts, :] = tok_dot(*TOK_RG)


def _inproj(x, ln1, wf, wt, qw_b, kw_b, tabs, layer, batch, seq):
    t = batch * seq
    tm = min(INPROJ_TILE, seq)
    nst = seq // tm
    cosd, sind, cosrt, sinrt, cosr, sinr = tabs
    lay3 = lambda i: (layer, 0, 0)
    in_specs = [
        pl.BlockSpec((tm, D_MODEL), lambda i: (i, 0)),
        pl.BlockSpec((None, 1, D_MODEL), lay3),
        pl.BlockSpec((None, FEAT_ROWS, D_MODEL), lay3),
        pl.BlockSpec((None, D_MODEL, TOK_COLS), lay3),
        pl.BlockSpec((None, QK_DIM, min(INPROJ_SUB, tm)), lay3),
        pl.BlockSpec((None, QK_DIM, min(INPROJ_SUB, tm)), lay3),
        pl.BlockSpec((8, tm), lambda i: (0, i % nst)),
        pl.BlockSpec((8, tm), lambda i: (0, i % nst)),
        pl.BlockSpec((32, tm), lambda i: (0, i % nst)),
        pl.BlockSpec((32, tm), lambda i: (0, i % nst)),
        pl.BlockSpec((tm, LANES), lambda i: (i % nst, 0)),
        pl.BlockSpec((tm, LANES), lambda i: (i % nst, 0)),
    ]
    feat_spec = pl.BlockSpec((None, 512, tm), lambda i: (i // nst, 0, i % nst))
    tok_spec = pl.BlockSpec((tm, 512), lambda i: (i, 0))
    out_specs = [
        feat_spec,
        tok_spec,
        pl.BlockSpec((None, DIFF_HEADS, DV_DIM, tm), lambda i: (i // nst, 0, 0, i % nst)),
        tok_spec,
        feat_spec,
        tok_spec,
        tok_spec,
    ]
    out_shape = [
        jax.ShapeDtypeStruct((batch, 512, seq), BF16),
        jax.ShapeDtypeStruct((t, 512), BF16),
        jax.ShapeDtypeStruct((batch, DIFF_HEADS, DV_DIM, seq), BF16),
        jax.ShapeDtypeStruct((t, 512), BF16),
        jax.ShapeDtypeStruct((batch, 512, seq), BF16),
        jax.ShapeDtypeStruct((t, 512), BF16),
        jax.ShapeDtypeStruct((t, 512), F32),
    ]
    return pl.pallas_call(
        _inproj_kernel, grid=(t // tm,), in_specs=in_specs, out_specs=out_specs, out_shape=out_shape,
        compiler_params=_cparams(1), name="inproj",
    )(x, ln1, wf, wt, qw_b, kw_b, cosd, sind, cosrt, sinrt, cosr, sinr)


def _attn_kernel(lp_ref, qw_ref, kw_ref, qt_ref, k_ref, v_ref, sw_ref, o_ref,
                 acc1, acc2, m1, m2, l1, l2, p_buf, *, lam_init, tk):
    seq = k_ref.shape[0]
    qt = qt_ref[...]
    row = lax.broadcasted_iota(jnp.int32, qt.shape, 0)
    zero = jnp.zeros_like(qt)
    qa = jnp.where(row < QK_DIM, qt, zero)
    qb = jnp.where(row >= QK_DIM, qt, zero)
    lp = lp_ref[...]
    lam = (jnp.exp(jnp.sum(lp[0:1] * lp[1:2], axis=1, keepdims=True))
           - jnp.exp(jnp.sum(lp[2:3] * lp[3:4], axis=1, keepdims=True)) + lam_init)

    def finish(out):
        oms = jnp.mean(out * out, axis=0, keepdims=True)
        y = out * lax.rsqrt(oms + EPS) * sw_ref[...]
        o_ref[...] = y.astype(BF16)

    bound = (jnp.max(jnp.abs(qw_ref[...]), axis=1, keepdims=True)
             * jnp.max(jnp.abs(kw_ref[...]), axis=1, keepdims=True) * (QK_DIM * BOUND_SLACK))
    kb = k_ref[...]
    row_sums = []
    for c, qx in enumerate((qa, qb)):
        e = jnp.exp2(jnp.dot(kb, qx, preferred_element_type=F32) - bound)
        row_sums.append(jnp.sum(e, axis=0, keepdims=True))
        p_buf[c] = e.astype(BF16)
    s1, s2 = row_sums
    ratio = (lam * s1 / s2).astype(BF16)
    w = p_buf[0] - ratio * p_buf[1]
    finish(jnp.dot(v_ref[...], w, preferred_element_type=F32) / s1)
    l_min = jnp.minimum(jnp.min(s1), jnp.min(s2))

    @pl.when(l_min < UNDERFLOW_GUARD)
    def _running_max_pass():
        comps = ((qa, acc1, m1, l1), (qb, acc2, m2, l2))
        for _, acc, m, l in comps:
            acc[...] = jnp.zeros_like(acc)
            l[...] = jnp.zeros_like(l)
            m[...] = jnp.full_like(m, NEG_BIG)

        def body(j, carry):
            off = pl.multiple_of(j * tk, tk)
            kblk = k_ref[pl.ds(off, tk), :]
            vblk = v_ref[:, pl.ds(off, tk)]
            for qx, acc, m, l in comps:
                s = jnp.dot(kblk, qx, preferred_element_type=F32)
                m_old = m[...]
                m_new = jnp.maximum(m_old, jnp.max(s, axis=0, keepdims=True))
                e = jnp.exp2(s - m_new)
                alpha = jnp.exp2(m_old - m_new)
                l[...] = l[...] * alpha + jnp.sum(e, axis=0, keepdims=True)
                acc[...] = acc[...] * alpha + jnp.dot(vblk, e.astype(BF16), preferred_element_type=F32)
                m[...] = m_new
            return carry

        lax.fori_loop(0, seq // tk, body, 0)
        finish(acc1[...] / l1[...] - lam * (acc2[...] / l2[...]))


def _diff_attention(lp, qw, kw, qt, k, v, sw_b, layer, lam_init, batch, seq):
    tq = min(Q_TILE, seq, ATTN_SCORE_ELEMS // seq)
    tk = min(KV_TILE, seq)
    k3 = k.reshape(batch, seq, DIFF_WIDTH)
    kern = functools.partial(_attn_kernel, lam_init=lam_init, tk=tk)
    lay3 = lambda b, h, i: (layer, 0, 0)
    return pl.pallas_call(
        kern,
        grid=(batch, DIFF_HEADS, seq // tq),
        in_specs=[
            pl.BlockSpec((None, 4, QK_DIM), lay3),
            pl.BlockSpec((None, 1, QK_DIM), lay3),
            pl.BlockSpec((None, 1, QK_DIM), lay3),
            pl.BlockSpec((None, DV_DIM, tq), lambda b, h, i: (b, h, i)),
            pl.BlockSpec((None, seq, DV_DIM), lambda b, h, i: (b, 0, h)),
            pl.BlockSpec((None, None, DV_DIM, seq), lambda b, h, i: (b, h, 0, 0)),
            pl.BlockSpec((None, DV_DIM, tq), lay3),
        ],
        out_specs=pl.BlockSpec((None, DV_DIM, tq), lambda b, h, i: (b, h, i)),
        out_shape=jax.ShapeDtypeStruct((batch, DIFF_WIDTH, seq), BF16),
        scratch_shapes=[pltpu.VMEM((DV_DIM, tq), F32), pltpu.VMEM((DV_DIM, tq), F32),
                        pltpu.VMEM((1, tq), F32), pltpu.VMEM((1, tq), F32),
                        pltpu.VMEM((1, tq), F32), pltpu.VMEM((1, tq), F32),
                        pltpu.VMEM((2, seq, tq), BF16)],
        compiler_params=_cparams(3), name="diff_attention",
    )(lp, qw, kw, qt, k3, v, sw_b)


def _ret_kernel(lgl_ref, lgr_ref, lgs_ref, gnw_ref, q_ref, kt_ref, v_ref, g_ref, o_ref,
                qdf, qdb, kdf, kdb, cdf, cdb, dcomb, gfull, s_st, t_st, t_all, o_buf, *, group):
    b = pl.program_id(0)
    ph = pl.program_id(1)
    j = pl.program_id(2)
    n_groups = pl.num_programs(2)
    c = CHUNK

    @pl.when((b == 0) & (ph == 0) & (j == 0))
    def _tables():
        lgl = jax.nn.log_sigmoid(lgl_ref[...])
        lgr = jax.nn.log_sigmoid(lgr_ref[...])
        lgs = jax.nn.log_sigmoid(lgs_ref[...])
        pos_row = lax.broadcasted_iota(jnp.int32, (c, RET_WIDTH), 0).astype(F32)
        qdf[...] = jnp.exp((pos_row + 1.0) * lgl[0, 0:1, :])
        qdb[...] = jnp.exp((c - pos_row) * lgl[1, 0:1, :])
        pos_lane = lax.broadcasted_iota(jnp.int32, (RET_WIDTH, c), 1).astype(F32)
        kdf[...] = jnp.exp((c - 1.0 - pos_lane) * lgr[0])
        kdb[...] = jnp.exp(pos_lane * lgr[1])
        cdf[...] = jnp.exp(float(c) * lgr[0])
        cdb[...] = jnp.exp(float(c) * lgr[1])
        qi = (lax.broadcasted_iota(jnp.int32, (RET_HEADS * c, c), 0) % c).astype(F32)
        ki = lax.broadcasted_iota(jnp.int32, (RET_HEADS * c, c), 1).astype(F32)
        rel = qi - ki
        dcomb[...] = jnp.where(rel >= 0, jnp.exp(rel * lgs[0]), jnp.exp(-rel * lgs[1]))

        frow = lax.broadcasted_iota(jnp.int32, (RET_WIDTH, RET_WIDTH), 0) // RET_HEAD_DIM
        fcol = lax.broadcasted_iota(jnp.int32, (RET_WIDTH, RET_WIDTH), 1) // RET_HEAD_DIM
        gfull[...] = jnp.where(frow == fcol, 1.0 / RET_HEAD_DIM, 0.0).astype(BF16)

    lane = lax.broadcasted_iota(jnp.int32, (c, c), 1)
    row = lax.broadcasted_iota(jnp.int32, (c, c), 0)
    lo = lane < RET_HEAD_DIM
    same_head = (row < RET_HEAD_DIM) == lo

    @pl.when((ph == 0) & (j == 0))
    def _init():
        t_st[...] = jnp.zeros_like(t_st)
        s_st[...] = jnp.zeros_like(s_st)

    @pl.when(ph == 0)
    def _backward_states():
        first = (n_groups - 1 - j) * group
        for p in range(RET_PAIRS):
            sl = slice(p * c, (p + 1) * c)
            t = t_st[p]
            for g in reversed(range(group)):
                gs = slice(g * c, (g + 1) * c)
                t_all[first + g, p] = t.astype(BF16)
                ktd = (kt_ref[sl, gs].astype(F32) * kdb[sl, :]).astype(BF16)
                kv = jnp.dot(ktd, v_ref[gs, sl], preferred_element_type=F32)
                t = t * cdb[sl, :] + jnp.where(same_head, kv, 0.0)
            t_st[p] = t

    @pl.when(ph == 1)
    def _forward():
        first = j * group
        combos = [(p, g) for p in range(RET_PAIRS) for g in range(group)]
        zq = jnp.zeros((c, c), BF16)

        def blk(p, g):
            return slice(g * c, (g + 1) * c), slice(p * c, (p + 1) * c)

        kvs, sds = {}, {}
        for p, g in combos:
            gs, sl = blk(p, g)
            ktp = kt_ref[sl, gs]
            ktd = (ktp.astype(F32) * kdf[sl, :]).astype(BF16)
            kvs[p, g] = jnp.dot(ktd, v_ref[gs, sl], preferred_element_type=F32)
            qp = q_ref[gs, sl]
            q2 = jnp.concatenate([jnp.where(lo, qp, zq), jnp.where(lo, zq, qp)], axis=0)
            s = jnp.dot(q2, ktp, preferred_element_type=F32)
            sds[p, g] = (s * dcomb[2 * p * c:2 * (p + 1) * c, :]).astype(BF16)
        states = {}
        for p in range(RET_PAIRS):
            sl = slice(p * c, (p + 1) * c)
            s_state = s_st[p]
            for g in range(group):
                states[p, g] = s_state.astype(BF16)
                s_state = s_state * cdf[sl, :] + jnp.where(same_head, kvs[p, g], 0.0)
            s_st[p] = s_state
        for p, g in combos:
            gs, sl = blk(p, g)
            qp = q_ref[gs, sl]
            vp = v_ref[gs, sl]
            qf = qp.astype(F32)
            sd = sds[p, g]
            lhs = jnp.concatenate([sd[0:c], sd[c:2 * c], (qf * qdf[:, sl]).astype(BF16),
                                   (qf * qdb[:, sl]).astype(BF16)], axis=1)
            rhs = jnp.concatenate([jnp.where(lo, vp, zq), jnp.where(lo, zq, vp),
                                   states[p, g], t_all[first + g, p]], axis=0)
            o_buf[gs, sl] = jnp.dot(lhs, rhs, preferred_element_type=F32)
        o = o_buf[...]
        gms = jnp.dot((o * o).astype(BF16), gfull[...], preferred_element_type=F32)
        r = o * lax.rsqrt(gms + EPS) * gnw_ref[0:1, :]
        gate = g_ref[...]
        o_ref[...] = (gate * jax.nn.sigmoid(gate) * r).astype(BF16)


def _retention(lg_lane, lg_row, lg_sc, gnw_b, rq, rkt, rv, rg, layer, batch, seq):
    c = CHUNK
    n = seq // c
    group = min(RET_GROUP, n)
    ng = n // group
    tg = group * c

    def tok_map(b, ph, j):
        return (b * ng + jnp.where(ph == 0, ng - 1 - j, j), 0)

    def feat_map(b, ph, j):
        return (b, 0, jnp.where(ph == 0, ng - 1 - j, j))

    def fwd_only_map(b, ph, j):
        return (b * ng + jnp.where(ph == 0, 0, j), 0)

    return pl.pallas_call(
        functools.partial(_ret_kernel, group=group),
        grid=(batch, 2, ng),
        in_specs=[
            pl.BlockSpec((None, 2, 8, RET_WIDTH), lambda b, ph, j: (layer, 0, 0, 0)),
            pl.BlockSpec((None, 2, RET_WIDTH, c), lambda b, ph, j: (layer, 0, 0, 0)),
            pl.BlockSpec((None, 2, RET_HEADS * c, c), lambda b, ph, j: (layer, 0, 0, 0)),
            pl.BlockSpec((None, 8, RET_WIDTH), lambda b, ph, j: (layer, 0, 0)),
            pl.BlockSpec((tg, RET_WIDTH), fwd_only_map),
            pl.BlockSpec((None, RET_WIDTH, tg), feat_map),
            pl.BlockSpec((tg, RET_WIDTH), tok_map),
            pl.BlockSpec((tg, RET_WIDTH), fwd_only_map),
        ],
        out_specs=pl.BlockSpec((tg, RET_WIDTH), fwd_only_map),
        out_shape=jax.ShapeDtypeStruct((batch * seq, RET_WIDTH), BF16),
        scratch_shapes=[
            pltpu.VMEM((c, RET_WIDTH), F32), pltpu.VMEM((c, RET_WIDTH), F32),
            pltpu.VMEM((RET_WIDTH, c), F32), pltpu.VMEM((RET_WIDTH, c), F32),
            pltpu.VMEM((RET_WIDTH, c), F32), pltpu.VMEM((RET_WIDTH, c), F32),
            pltpu.VMEM((RET_HEADS * c, c), F32),
            pltpu.VMEM((RET_WIDTH, RET_WIDTH), BF16),
            pltpu.VMEM((RET_PAIRS, c, c), F32), pltpu.VMEM((RET_PAIRS, c, c), F32),
            pltpu.VMEM((n, RET_PAIRS, c, c), BF16),
            pltpu.VMEM((tg, RET_WIDTH), F32),
        ],
        compiler_params=_cparams(3), name="retention",
    )(lg_lane, lg_row, lg_sc, gnw_b, rq, rkt, rv, rg)


def _post_kernel(x_ref, a_ref, r_ref, p_ref, wo_ref, ln2_ref, w1_ref, w2_ref, wg_ref, wp_ref, o_ref):
    sub = min(POST_SUB, x_ref.shape[0])
    for part in range(x_ref.shape[0] // sub):
        ts = slice(part * sub, (part + 1) * sub)
        x = x_ref[ts, :]
        a_tok = a_ref[:, ts].astype(F32).T.astype(BF16)
        x = x + jnp.dot(a_tok, wo_ref[0:DIFF_WIDTH, :], preferred_element_type=F32)
        x = x + jnp.dot(r_ref[ts, :], wo_ref[DIFF_WIDTH:, :], preferred_element_type=F32)
        ms = jnp.mean(x * x, axis=-1, keepdims=True)
        h2 = (x * lax.rsqrt(ms + EPS) * ln2_ref[...]).astype(BF16)
        hid = jnp.dot(h2, w1_ref[...], preferred_element_type=F32)
        act = jnp.square(jnp.maximum(hid, 0.0)).astype(BF16)
        x = x + jnp.dot(act, w2_ref[...], preferred_element_type=F32)
        gate = jax.nn.sigmoid(jnp.dot(x.astype(BF16), wg_ref[...], preferred_element_type=F32))
        pe = jnp.dot(p_ref[ts, :].astype(BF16), wp_ref[...], preferred_element_type=F32)
        o_ref[ts, :] = x + gate * pe


def _post(x, a, r, p, wo, ln2, w1, w2, wg, wp, layer, batch, seq):
    t = batch * seq
    tm = min(TOKEN_TILE, seq)
    nst = seq // tm
    lay3 = lambda i: (layer, 0, 0)

    def wspec(shape):
        return pl.BlockSpec((None,) + shape, lay3, pipeline_mode=pl.Buffered(1))

    return pl.pallas_call(
        _post_kernel,
        grid=(t // tm,),
        in_specs=[
            pl.BlockSpec((tm, D_MODEL), lambda i: (i, 0)),
            pl.BlockSpec((None, DIFF_WIDTH, tm), lambda i: (i // nst, 0, i % nst)),
            pl.BlockSpec((tm, RET_WIDTH), lambda i: (i, 0)),
            pl.BlockSpec((None, tm, PLE_DIM), lambda i: (layer, i, 0)),
            wspec((D_MODEL, D_MODEL)),
            pl.BlockSpec((None, 1, D_MODEL), lay3),
            wspec((D_MODEL, D_FF)),
            wspec((D_FF, D_MODEL)),
            wspec((D_MODEL, D_MODEL)),
            wspec((PLE_DIM, D_MODEL)),
        ],
        out_specs=pl.BlockSpec((tm, D_MODEL), lambda i: (i, 0)),
        out_shape=jax.ShapeDtypeStruct((t, D_MODEL), F32),
        compiler_params=_cparams(1), name="post",
    )(x, a, r, p, wo, ln2, w1, w2, wg, wp)


def _prepare_params(ln1_w, w_in, diff_q_norm, diff_k_norm, diff_subln, ret_decay_logit, ret_gn,
                    w_out, ln2_w, w_mlp1, w_mlp2, w_ple_gate, w_ple_proj):
    depth = w_in.shape[0]
    tm = INPROJ_SUB
    dq, dk, dv, rq, rk, rv, rg = (w_in[:, :, i * 512:(i + 1) * 512] for i in range(7))
    wf = jnp.swapaxes(jnp.concatenate([dq, dk, dv, rk], axis=2), 1, 2).astype(BF16)
    wt = jnp.concatenate([rq, rv, rg], axis=2).astype(BF16)
    q_scale = (QK_DIM ** -0.5) * LOG2E
    qw_b = jnp.broadcast_to((diff_q_norm.astype(F32) * q_scale)[:, :, None], (depth, QK_DIM, tm))
    kw_b = jnp.broadcast_to(diff_k_norm.astype(F32)[:, :, None], (depth, QK_DIM, tm))
    lam_init = [0.8 - 0.6 * math.exp(-0.3 * i) for i in range(depth)]
    sub_scale = jnp.asarray([1.0 - li for li in lam_init], F32)[:, None]
    sw_b = jnp.broadcast_to((diff_subln.astype(F32) * sub_scale)[:, :, None], (depth, DV_DIM, Q_TILE))
    lgt = ret_decay_logit.astype(F32)
    lg_lane = jnp.broadcast_to(jnp.repeat(lgt, RET_HEAD_DIM, axis=2)[:, :, None, :], (depth, 2, 8, RET_WIDTH))
    lg_row = jnp.broadcast_to(jnp.repeat(lgt, RET_HEAD_DIM, axis=2)[:, :, :, None], (depth, 2, RET_WIDTH, CHUNK))
    lg_sc = jnp.broadcast_to(jnp.repeat(lgt, CHUNK, axis=2)[:, :, :, None], (depth, 2, RET_HEADS * CHUNK, CHUNK))
    gnw_b = jnp.broadcast_to(jnp.tile(ret_gn.astype(F32), (1, RET_HEADS))[:, None, :], (depth, 8, RET_WIDTH))
    return dict(
        ln1=ln1_w.astype(F32)[:, None, :], wf=wf, wt=wt, qw_b=qw_b, kw_b=kw_b, sw_b=sw_b, lam_init=lam_init,
        qw=(diff_q_norm.astype(F32) * q_scale)[:, None, :], kw=diff_k_norm.astype(F32)[:, None, :],
        lg_lane=lg_lane, lg_row=lg_row, lg_sc=lg_sc, gnw_b=gnw_b,
        wo=w_out.astype(BF16), ln2=ln2_w.astype(F32)[:, None, :], w1=w_mlp1.astype(BF16),
        w2=w_mlp2.astype(BF16), wg=w_ple_gate.astype(BF16), wp=w_ple_proj.astype(BF16))


def _run_trunk(x, p, prm, tabs, diff_lambda):
    batch, seq, _ = x.shape
    depth = p.shape[0]
    xf = x.reshape(batch * seq, D_MODEL)
    pf = p.reshape(depth, batch * seq, PLE_DIM)
    lp = diff_lambda.astype(F32)
    for i in range(depth):
        qt, k, v, rq, rkt, rv, rg = _inproj(xf, prm["ln1"], prm["wf"], prm["wt"], prm["qw_b"], prm["kw_b"],
                                            tabs, i, batch, seq)
        a = _diff_attention(lp, prm["qw"], prm["kw"], qt, k, v, prm["sw_b"], i, prm["lam_init"][i],
                            batch, seq)
        r = _retention(prm["lg_lane"], prm["lg_row"], prm["lg_sc"], prm["gnw_b"], rq, rkt, rv, rg,
                       i, batch, seq)
        xf = _post(xf, a, r, pf, prm["wo"], prm["ln2"], prm["w1"],
                   prm["w2"], prm["wg"], prm["wp"], i, batch, seq)
    return xf.reshape(batch, seq, D_MODEL)


def kernel(x_prompt, x_sample, p_prompt, p_sample, ln1_w, w_in, diff_q_norm, diff_k_norm, diff_lambda,
           diff_subln, ret_decay_logit, ret_gn, w_out, ln2_w, w_mlp1, w_mlp2, w_ple_gate, w_ple_proj):
    prm = _prepare_params(ln1_w, w_in, diff_q_norm, diff_k_norm, diff_subln, ret_decay_logit, ret_gn,
                          w_out, ln2_w, w_mlp1, w_mlp2, w_ple_gate, w_ple_proj)
    tabs = _rope_tables(max(x_prompt.shape[1], x_sample.shape[1]))
    y_prompt = _run_trunk(x_prompt, p_prompt, prm, tabs, diff_lambda)
    y_sample = _run_trunk(x_sample, p_sample, prm, tabs, diff_lambda)
    return (y_prompt, y_sample)
```

```python
import functools
import math

import jax
import jax.numpy as jnp
from jax import lax
from jax.experimental import pallas as pl
from jax.experimental.pallas import tpu as pltpu

F32 = jnp.float32
BF16 = jnp.bfloat16

D_MODEL = 1024
DIFF_WIDTH = 512
RET_WIDTH = 512
QK_DIM = 64
DV_DIM = 128
DIFF_HEADS = 4
RET_HEAD_DIM = 64
RET_HEADS = 8
RET_PAIRS = RET_HEADS // 2
ROT_DIMS = 16
ROPE_THETA = 500000.0
RET_THETA = 10000.0
D_FF = 4096
PLE_DIM = 256
CHUNK = 128
EPS = 1e-6
LOG2E = 1.4426950408889634
NEG_BIG = -1e30

V7X_VMEM_BYTES = 64 * 1024 * 1024
VMEM_LIMIT_BYTES = V7X_VMEM_BYTES - 8 * 1024 * 1024
LANES = 128

FEAT_DQ, FEAT_DK, FEAT_DV, FEAT_RK = ((i * 512, (i + 1) * 512) for i in range(4))
TOK_RQ, TOK_RV, TOK_RG = ((i * 512, (i + 1) * 512) for i in range(3))
FEAT_ROWS = FEAT_RK[1]
TOK_COLS = TOK_RG[1]

TOKEN_TILE = 1024
POST_SUB = 512
INPROJ_TILE = 1024
INPROJ_SUB = 256
Q_TILE = 1024
KV_TILE = 256
ATTN_SCORE_ELEMS = 4096 * 1024
BOUND_SLACK = 1.02
UNDERFLOW_GUARD = 2.0 ** -64
RET_GROUP = 16
TABLE_TILE = 1024


def _cparams(n_grid):
    return pltpu.CompilerParams(dimension_semantics=("arbitrary",) * n_grid,
                                vmem_limit_bytes=VMEM_LIMIT_BYTES)


def _const_spec(shape):
    nd = len(shape)
    return pl.BlockSpec(shape, lambda *_: (0,) * nd)


def _rope_kernel(invd_ref, invrt_ref, invr_ref, sgn_ref,
                 cosd_ref, sind_ref, cosrt_ref, sinrt_ref, cosr_ref, sinr_ref):
    ts = cosd_ref.shape[1]
    base = pl.program_id(0) * ts
    pos_d = (base + lax.broadcasted_iota(jnp.int32, (8, ts), 1)).astype(F32)
    ang_d = pos_d * invd_ref[...]
    cosd_ref[...] = jnp.cos(ang_d)
    sind_ref[...] = jnp.sin(ang_d)
    pos_r = (base + lax.broadcasted_iota(jnp.int32, (32, ts), 1)).astype(F32)
    ang_r = pos_r * invrt_ref[...]
    cosrt_ref[...] = jnp.cos(ang_r)
    sinrt_ref[...] = jnp.sin(ang_r)
    pos_t = (base + lax.broadcasted_iota(jnp.int32, (ts, LANES), 0)).astype(F32)
    ang_t = pos_t * invr_ref[0:1, :]
    cosr_ref[...] = jnp.cos(ang_t)
    sinr_ref[...] = jnp.sin(ang_t) * sgn_ref[0:1, :]


def _rope_tables(s_max):
    ts = min(TABLE_TILE, s_max)
    inv_d = 1.0 / (ROPE_THETA ** (jnp.arange(0, ROT_DIMS, 2, dtype=F32) / ROT_DIMS))
    inv_r = 1.0 / (RET_THETA ** (jnp.arange(0, RET_HEAD_DIM, 2, dtype=F32) / RET_HEAD_DIM))
    invd_b = jnp.broadcast_to(inv_d[:, None], (8, ts))
    invrt_b = jnp.broadcast_to(inv_r[:, None], (32, ts))
    invr_lane = jnp.broadcast_to(jnp.tile(inv_r, LANES // 32)[None, :], (8, LANES))
    sgn = jnp.where((jnp.arange(LANES) % RET_HEAD_DIM) < RET_HEAD_DIM // 2, -1.0, 1.0).astype(F32)
    sgn_b = jnp.broadcast_to(sgn[None, :], (8, LANES))
    return pl.pallas_call(
        _rope_kernel,
        grid=(s_max // ts,),
        in_specs=[_const_spec((8, ts)), _const_spec((32, ts)), _const_spec((8, LANES)),
                  _const_spec((8, LANES))],
        out_specs=[pl.BlockSpec((8, ts), lambda i: (0, i)), pl.BlockSpec((8, ts), lambda i: (0, i)),
                   pl.BlockSpec((32, ts), lambda i: (0, i)), pl.BlockSpec((32, ts), lambda i: (0, i)),
                   pl.BlockSpec((ts, LANES), lambda i: (i, 0)), pl.BlockSpec((ts, LANES), lambda i: (i, 0))],
        out_shape=[jax.ShapeDtypeStruct((8, s_max), F32), jax.ShapeDtypeStruct((8, s_max), F32),
                   jax.ShapeDtypeStruct((32, s_max), F32), jax.ShapeDtypeStruct((32, s_max), F32),
                   jax.ShapeDtypeStruct((s_max, LANES), F32), jax.ShapeDtypeStruct((s_max, LANES), F32)],
        compiler_params=_cparams(1),
        name="rope_tables",
    )(invd_b, invrt_b, invr_lane, sgn_b)


def _inproj_kernel(x_ref, ln_ref, wf_ref, wt_ref, qw_ref, kw_ref,
                   cd_ref, sd_ref, crt_ref, srt_ref, cr_ref, sr_ref,
                   qt_ref, k_ref, v_ref, rq_ref, rkt_ref, rv_ref, rg_ref):
    tm = min(INPROJ_SUB, x_ref.shape[0])
    half = RET_HEAD_DIM // 2

    def norm_rot(z, w, cd, sd):
        z3 = z.reshape(2 * DIFF_HEADS, QK_DIM, tm)
        gms = jnp.mean(z3 * z3, axis=1, keepdims=True)
        y = z3 * lax.rsqrt(gms + EPS) * w[None]
        y1 = y[:, 0:8]
        y2 = y[:, 8:16]
        out = jnp.concatenate([y1 * cd - y2 * sd, y1 * sd + y2 * cd, y[:, 16:]], axis=1)
        return out.reshape(DIFF_WIDTH, tm)

    for part in range(x_ref.shape[0] // tm):
        ts = slice(part * tm, (part + 1) * tm)
        x = x_ref[ts, :]
        ms = jnp.mean(x * x, axis=-1, keepdims=True)
        hn = (x * lax.rsqrt(ms + EPS) * ln_ref[...]).astype(BF16)

        def feat_dot(lo, hi):
            return lax.dot_general(wf_ref[lo:hi, :], hn, (((1,), (1,)), ((), ())),
                                   preferred_element_type=F32)

        def tok_dot(lo, hi):
            return jnp.dot(hn, wt_ref[:, lo:hi], preferred_element_type=F32)

        cd = cd_ref[:, ts]
        sd = sd_ref[:, ts]
        k_ref[ts, :] = norm_rot(feat_dot(*FEAT_DK), kw_ref[...], cd, sd).T.astype(BF16)
        qt_ref[:, ts] = norm_rot(feat_dot(*FEAT_DQ), qw_ref[...], cd, sd).astype(BF16)

        rq = tok_dot(*TOK_RQ)
        lane = lax.broadcasted_iota(jnp.int32, rq.shape, 1)
        partner = jnp.where((lane % RET_HEAD_DIM) < half,
                            pltpu.roll(rq, RET_WIDTH - half, 1), pltpu.roll(rq, half, 1))
        cr = jnp.concatenate([cr_ref[ts, :]] * (RET_WIDTH // LANES), axis=1)
        sr = jnp.concatenate([sr_ref[ts, :]] * (RET_WIDTH // LANES), axis=1)
        rq_ref[ts, :] = (rq * cr + partner * sr).astype(BF16)

        rk3 = feat_dot(*FEAT_RK).reshape(RET_HEADS, RET_HEAD_DIM, tm)
        crt = crt_ref[:, ts]
        srt = srt_ref[:, ts]
        r1 = rk3[:, 0:half]
        r2 = rk3[:, half:]
        rk_rot = jnp.concatenate([r1 * crt - r2 * srt, r1 * srt + r2 * crt], axis=1)
        rkt_ref[:, ts] = (rk_rot * (RET_HEAD_DIM ** -0.5)).reshape(RET_WIDTH, tm).astype(BF16)

        v_ref[:, :, ts] = feat_dot(*FEAT_DV).reshape(DIFF_HEADS, DV_DIM, tm).astype(BF16)
        rv_ref[ts, :] = tok_dot(*TOK_RV).astype(BF16)
        rg_ref[ts, :] = tok_dot(*TOK_RG)


def _inproj(x, ln1, wf, wt, qw_b, kw_b, tabs, layer, batch, seq):
    t = batch * seq
    tm = min(INPROJ_TILE, seq)
    nst = seq // tm
    cosd, sind, cosrt, sinrt, cosr, sinr = tabs
    lay3 = lambda i: (layer, 0, 0)
    in_specs = [
        pl.BlockSpec((tm, D_MODEL), lambda i: (i, 0)),
        pl.BlockSpec((None, 1, D_MODEL), lay3),
        pl.BlockSpec((None, FEAT_ROWS, D_MODEL), lay3),
        pl.BlockSpec((None, D_MODEL, TOK_COLS), lay3),
        pl.BlockSpec((None, QK_DIM, min(INPROJ_SUB, tm)), lay3),
        pl.BlockSpec((None, QK_DIM, min(INPROJ_SUB, tm)), lay3),
        pl.BlockSpec((8, tm), lambda i: (0, i % nst)),
        pl.BlockSpec((8, tm), lambda i: (0, i % nst)),
        pl.BlockSpec((32, tm), lambda i: (0, i % nst)),
        pl.BlockSpec((32, tm), lambda i: (0, i % nst)),
        pl.BlockSpec((tm, LANES), lambda i: (i % nst, 0)),
        pl.BlockSpec((tm, LANES), lambda i: (i % nst, 0)),
    ]
    feat_spec = pl.BlockSpec((None, 512, tm), lambda i: (i // nst, 0, i % nst))
    tok_spec = pl.BlockSpec((tm, 512), lambda i: (i, 0))
    out_specs = [
        feat_spec,
        tok_spec,
        pl.BlockSpec((None, DIFF_HEADS, DV_DIM, tm), lambda i: (i // nst, 0, 0, i % nst)),
        tok_spec,
        feat_spec,
        tok_spec,
        tok_spec,
    ]
    out_shape = [
        jax.ShapeDtypeStruct((batch, 512, seq), BF16),
        jax.ShapeDtypeStruct((t, 512), BF16),
        jax.ShapeDtypeStruct((batch, DIFF_HEADS, DV_DIM, seq), BF16),
        jax.ShapeDtypeStruct((t, 512), BF16),
        jax.ShapeDtypeStruct((batch, 512, seq), BF16),
        jax.ShapeDtypeStruct((t, 512), BF16),
        jax.ShapeDtypeStruct((t, 512), F32),
    ]
    return pl.pallas_call(
        _inproj_kernel, grid=(t // tm,), in_specs=in_specs, out_specs=out_specs, out_shape=out_shape,
        compiler_params=pltpu.CompilerParams(
            dimension_semantics=("arbitrary",), vmem_limit_bytes=VMEM_LIMIT_BYTES,
            allow_input_fusion=[False, False, True, True] + [False] * 8),
        name="inproj",
    )(x, ln1, wf, wt, qw_b, kw_b, cosd, sind, cosrt, sinrt, cosr, sinr)


def _attn_kernel(lp_ref, qw_ref, kw_ref, qt_ref, k_ref, v_ref, sw_ref, o_ref,
                 acc1, acc2, m1, m2, l1, l2, p_buf, *, lam_init, tk):
    seq = k_ref.shape[0]
    qt = qt_ref[...]
    row = lax.broadcasted_iota(jnp.int32, qt.shape, 0)
    zero = jnp.zeros_like(qt)
    qa = jnp.where(row < QK_DIM, qt, zero)
    qb = jnp.where(row >= QK_DIM, qt, zero)
    lp = lp_ref[...]
    lam = (jnp.exp(jnp.sum(lp[0:1] * lp[1:2], axis=1, keepdims=True))
           - jnp.exp(jnp.sum(lp[2:3] * lp[3:4], axis=1, keepdims=True)) + lam_init)

    def finish(out):
        oms = jnp.mean(out * out, axis=0, keepdims=True)
        y = out * lax.rsqrt(oms + EPS) * sw_ref[...]
        o_ref[...] = y.astype(BF16)

    bound = (jnp.max(jnp.abs(qw_ref[...]), axis=1, keepdims=True)
             * jnp.max(jnp.abs(kw_ref[...]), axis=1, keepdims=True) * (QK_DIM * BOUND_SLACK))
    kb = k_ref[...]
    row_sums = []
    for c, qx in enumerate((qa, qb)):
        e = jnp.exp2(jnp.dot(kb, qx, preferred_element_type=F32) - bound)
        row_sums.append(jnp.sum(e, axis=0, keepdims=True))
        p_buf[c] = e.astype(BF16)
    s1, s2 = row_sums
    ratio = (lam * s1 / s2).astype(BF16)
    w = p_buf[0] - ratio * p_buf[1]
    finish(jnp.dot(v_ref[...], w, preferred_element_type=F32) / s1)
    l_min = jnp.minimum(jnp.min(s1), jnp.min(s2))

    @pl.when(l_min < UNDERFLOW_GUARD)
    def _running_max_pass():
        comps = ((qa, acc1, m1, l1), (qb, acc2, m2, l2))
        for _, acc, m, l in comps:
            acc[...] = jnp.zeros_like(acc)
            l[...] = jnp.zeros_like(l)
            m[...] = jnp.full_like(m, NEG_BIG)

        def body(j, carry):
            off = pl.multiple_of(j * tk, tk)
            kblk = k_ref[pl.ds(off, tk), :]
            vblk = v_ref[:, pl.ds(off, tk)]
            for qx, acc, m, l in comps:
                s = jnp.dot(kblk, qx, preferred_element_type=F32)
                m_old = m[...]
                m_new = jnp.maximum(m_old, jnp.max(s, axis=0, keepdims=True))
                e = jnp.exp2(s - m_new)
                alpha = jnp.exp2(m_old - m_new)
                l[...] = l[...] * alpha + jnp.sum(e, axis=0, keepdims=True)
                acc[...] = acc[...] * alpha + jnp.dot(vblk, e.astype(BF16), preferred_element_type=F32)
                m[...] = m_new
            return carry

        lax.fori_loop(0, seq // tk, body, 0)
        finish(acc1[...] / l1[...] - lam * (acc2[...] / l2[...]))


def _diff_attention(lp, qw, kw, qt, k, v, sw_b, layer, lam_init, batch, seq):
    tq = min(Q_TILE, seq, ATTN_SCORE_ELEMS // seq)
    tk = min(KV_TILE, seq)
    k3 = k.reshape(batch, seq, DIFF_WIDTH)
    kern = functools.partial(_attn_kernel, lam_init=lam_init, tk=tk)
    lay3 = lambda b, h, i: (layer, 0, 0)
    return pl.pallas_call(
        kern,
        grid=(batch, DIFF_HEADS, seq // tq),
        in_specs=[
            pl.BlockSpec((None, 4, QK_DIM), lay3),
            pl.BlockSpec((None, 1, QK_DIM), lay3),
            pl.BlockSpec((None, 1, QK_DIM), lay3),
            pl.BlockSpec((None, DV_DIM, tq), lambda b, h, i: (b, h, i)),
            pl.BlockSpec((None, seq, DV_DIM), lambda b, h, i: (b, 0, h)),
            pl.BlockSpec((None, None, DV_DIM, seq), lambda b, h, i: (b, h, 0, 0)),
            pl.BlockSpec((None, DV_DIM, tq), lay3),
        ],
        out_specs=pl.BlockSpec((None, DV_DIM, tq), lambda b, h, i: (b, h, i)),
        out_shape=jax.ShapeDtypeStruct((batch, DIFF_WIDTH, seq), BF16),
        scratch_shapes=[pltpu.VMEM((DV_DIM, tq), F32), pltpu.VMEM((DV_DIM, tq), F32),
                        pltpu.VMEM((1, tq), F32), pltpu.VMEM((1, tq), F32),
                        pltpu.VMEM((1, tq), F32), pltpu.VMEM((1, tq), F32),
                        pltpu.VMEM((2, seq, tq), BF16)],
        compiler_params=_cparams(3), name="diff_attention",
    )(lp, qw, kw, qt, k3, v, sw_b)


def _ret_kernel(lgl_ref, lgr_ref, lgs_ref, gnw_ref, q_ref, kt_ref, v_ref, g_ref, o_ref,
                qdf, qdb, kdf, kdb, cdf, cdb, dcomb, gfull, s_st, t_st, t_all, o_buf, *, group):
    b = pl.program_id(0)
    ph = pl.program_id(1)
    j = pl.program_id(2)
    n_groups = pl.num_programs(2)
    c = CHUNK

    @pl.when((b == 0) & (ph == 0) & (j == 0))
    def _tables():
        lgl = jax.nn.log_sigmoid(lgl_ref[...])
        lgr = jax.nn.log_sigmoid(lgr_ref[...])
        lgs = jax.nn.log_sigmoid(lgs_ref[...])
        pos_row = lax.broadcasted_iota(jnp.int32, (c, RET_WIDTH), 0).astype(F32)
        qdf[...] = jnp.exp((pos_row + 1.0) * lgl[0, 0:1, :])
        qdb[...] = jnp.exp((c - pos_row) * lgl[1, 0:1, :])
        pos_lane = lax.broadcasted_iota(jnp.int32, (RET_WIDTH, c), 1).astype(F32)
        kdf[...] = jnp.exp((c - 1.0 - pos_lane) * lgr[0])
        kdb[...] = jnp.exp(pos_lane * lgr[1])
        cdf[...] = jnp.exp(float(c) * lgr[0])
        cdb[...] = jnp.exp(float(c) * lgr[1])
        qi = (lax.broadcasted_iota(jnp.int32, (RET_HEADS * c, c), 0) % c).astype(F32)
        ki = lax.broadcasted_iota(jnp.int32, (RET_HEADS * c, c), 1).astype(F32)
        rel = qi - ki
        dcomb[...] = jnp.where(rel >= 0, jnp.exp(rel * lgs[0]), jnp.exp(-rel * lgs[1]))

        frow = lax.broadcasted_iota(jnp.int32, (RET_WIDTH, RET_WIDTH), 0) // RET_HEAD_DIM
        fcol = lax.broadcasted_iota(jnp.int32, (RET_WIDTH, RET_WIDTH), 1) // RET_HEAD_DIM
        gfull[...] = jnp.where(frow == fcol, 1.0 / RET_HEAD_DIM, 0.0).astype(BF16)

    lane = lax.broadcasted_iota(jnp.int32, (c, c), 1)
    row = lax.broadcasted_iota(jnp.int32, (c, c), 0)
    lo = lane < RET_HEAD_DIM
    same_head = (row < RET_HEAD_DIM) == lo

    @pl.when((ph == 0) & (j == 0))
    def _init():
        t_st[...] = jnp.zeros_like(t_st)
        s_st[...] = jnp.zeros_like(s_st)

    @pl.when(ph == 0)
    def _backward_states():
        first = (n_groups - 1 - j) * group
        for p in range(RET_PAIRS):
            sl = slice(p * c, (p + 1) * c)
            t = t_st[p]
            for g in reversed(range(group)):
                gs = slice(g * c, (g + 1) * c)
                t_all[first + g, p] = t.astype(BF16)
                ktd = (kt_ref[sl, gs].astype(F32) * kdb[sl, :]).astype(BF16)
                kv = jnp.dot(ktd, v_ref[gs, sl], preferred_element_type=F32)
                t = t * cdb[sl, :] + jnp.where(same_head, kv, 0.0)
            t_st[p] = t

    @pl.when(ph == 1)
    def _forward():
        first = j * group
        combos = [(p, g) for p in range(RET_PAIRS) for g in range(group)]
        zq = jnp.zeros((c, c), BF16)

        def blk(p, g):
            return slice(g * c, (g + 1) * c), slice(p * c, (p + 1) * c)

        kvs, sds = {}, {}
        for p, g in combos:
            gs, sl = blk(p, g)
            ktp = kt_ref[sl, gs]
            ktd = (ktp.astype(F32) * kdf[sl, :]).astype(BF16)
            kvs[p, g] = jnp.dot(ktd, v_ref[gs, sl], preferred_element_type=F32)
            qp = q_ref[gs, sl]
            q2 = jnp.concatenate([jnp.where(lo, qp, zq), jnp.where(lo, zq, qp)], axis=0)
            s = jnp.dot(q2, ktp, preferred_element_type=F32)
            sds[p, g] = (s * dcomb[2 * p * c:2 * (p + 1) * c, :]).astype(BF16)
        states = {}
        for p in range(RET_PAIRS):
            sl = slice(p * c, (p + 1) * c)
            s_state = s_st[p]
            for g in range(group):
                states[p, g] = s_state.astype(BF16)
                s_state = s_state * cdf[sl, :] + jnp.where(same_head, kvs[p, g], 0.0)
            s_st[p] = s_state
        for p, g in combos:
            gs, sl = blk(p, g)
            qp = q_ref[gs, sl]
            vp = v_ref[gs, sl]
            qf = qp.astype(F32)
            sd = sds[p, g]
            lhs = jnp.concatenate([sd[0:c], sd[c:2 * c], (qf * qdf[:, sl]).astype(BF16),
                                   (qf * qdb[:, sl]).astype(BF16)], axis=1)
            rhs = jnp.concatenate([jnp.where(lo, vp, zq), jnp.where(lo, zq, vp),
                                   states[p, g], t_all[first + g, p]], axis=0)
            o_buf[gs, sl] = jnp.dot(lhs, rhs, preferred_element_type=F32)
        o = o_buf[...]
        gms = jnp.dot((o * o).astype(BF16), gfull[...], preferred_element_type=F32)
        r = o * lax.rsqrt(gms + EPS) * gnw_ref[0:1, :]
        gate = g_ref[...]
        o_ref[...] = (gate * jax.nn.sigmoid(gate) * r).astype(BF16)


def _retention(lg_lane, lg_row, lg_sc, gnw_b, rq, rkt, rv, rg, layer, batch, seq):
    c = CHUNK
    n = seq // c
    group = min(RET_GROUP, n)
    ng = n // group
    tg = group * c

    def tok_map(b, ph, j):
        return (b * ng + jnp.where(ph == 0, ng - 1 - j, j), 0)

    def feat_map(b, ph, j):
        return (b, 0, jnp.where(ph == 0, ng - 1 - j, j))

    def fwd_only_map(b, ph, j):
        return (b * ng + jnp.where(ph == 0, 0, j), 0)

    return pl.pallas_call(
        functools.partial(_ret_kernel, group=group),
        grid=(batch, 2, ng),
        in_specs=[
            pl.BlockSpec((None, 2, 8, RET_WIDTH), lambda b, ph, j: (layer, 0, 0, 0)),
            pl.BlockSpec((None, 2, RET_WIDTH, c), lambda b, ph, j: (layer, 0, 0, 0)),
            pl.BlockSpec((None, 2, RET_HEADS * c, c), lambda b, ph, j: (layer, 0, 0, 0)),
            pl.BlockSpec((None, 8, RET_WIDTH), lambda b, ph, j: (layer, 0, 0)),
            pl.BlockSpec((tg, RET_WIDTH), fwd_only_map),
            pl.BlockSpec((None, RET_WIDTH, tg), feat_map),
            pl.BlockSpec((tg, RET_WIDTH), tok_map),
            pl.BlockSpec((tg, RET_WIDTH), fwd_only_map),
        ],
        out_specs=pl.BlockSpec((tg, RET_WIDTH), fwd_only_map),
        out_shape=jax.ShapeDtypeStruct((batch * seq, RET_WIDTH), BF16),
        scratch_shapes=[
            pltpu.VMEM((c, RET_WIDTH), F32), pltpu.VMEM((c, RET_WIDTH), F32),
            pltpu.VMEM((RET_WIDTH, c), F32), pltpu.VMEM((RET_WIDTH, c), F32),
            pltpu.VMEM((RET_WIDTH, c), F32), pltpu.VMEM((RET_WIDTH, c), F32),
            pltpu.VMEM((RET_HEADS * c, c), F32),
            pltpu.VMEM((RET_WIDTH, RET_WIDTH), BF16),
            pltpu.VMEM((RET_PAIRS, c, c), F32), pltpu.VMEM((RET_PAIRS, c, c), F32),
            pltpu.VMEM((n, RET_PAIRS, c, c), BF16),
            pltpu.VMEM((tg, RET_WIDTH), F32),
        ],
        compiler_params=_cparams(3), name="retention",
    )(lg_lane, lg_row, lg_sc, gnw_b, rq, rkt, rv, rg)


def _post_kernel(x_ref, a_ref, r_ref, p_ref, wo_ref, ln2_ref, w1_ref, w2_ref, wg_ref, wp_ref, o_ref):
    sub = min(POST_SUB, x_ref.shape[0])
    for part in range(x_ref.shape[0] // sub):
        ts = slice(part * sub, (part + 1) * sub)
        x = x_ref[ts, :]
        a_tok = a_ref[:, ts].astype(F32).T.astype(BF16)
        x = x + jnp.dot(a_tok, wo_ref[0:DIFF_WIDTH, :], preferred_element_type=F32)
        x = x + jnp.dot(r_ref[ts, :], wo_ref[DIFF_WIDTH:, :], preferred_element_type=F32)
        ms = jnp.mean(x * x, axis=-1, keepdims=True)
        h2 = (x * lax.rsqrt(ms + EPS) * ln2_ref[...]).astype(BF16)
        hid = jnp.dot(h2, w1_ref[...], preferred_element_type=F32)
        act = jnp.square(jnp.maximum(hid, 0.0)).astype(BF16)
        x = x + jnp.dot(act, w2_ref[...], preferred_element_type=F32)
        gate = jax.nn.sigmoid(jnp.dot(x.astype(BF16), wg_ref[...], preferred_element_type=F32))
        pe = jnp.dot(p_ref[ts, :].astype(BF16), wp_ref[...], preferred_element_type=F32)
        o_ref[ts, :] = x + gate * pe


def _post(x, a, r, p, wo, ln2, w1, w2, wg, wp, layer, batch, seq):
    t = batch * seq
    tm = min(TOKEN_TILE, seq)
    nst = seq // tm
    lay3 = lambda i: (layer, 0, 0)

    def wspec(shape):
        return pl.BlockSpec((None,) + shape, lay3, pipeline_mode=pl.Buffered(1))

    return pl.pallas_call(
        _post_kernel,
        grid=(t // tm,),
        in_specs=[
            pl.BlockSpec((tm, D_MODEL), lambda i: (i, 0)),
            pl.BlockSpec((None, DIFF_WIDTH, tm), lambda i: (i // nst, 0, i % nst)),
            pl.BlockSpec((tm, RET_WIDTH), lambda i: (i, 0)),
            pl.BlockSpec((None, tm, PLE_DIM), lambda i: (layer, i, 0)),
            wspec((D_MODEL, D_MODEL)),
            pl.BlockSpec((None, 1, D_MODEL), lay3),
            wspec((D_MODEL, D_FF)),
            wspec((D_FF, D_MODEL)),
            wspec((D_MODEL, D_MODEL)),
            wspec((PLE_DIM, D_MODEL)),
        ],
        out_specs=pl.BlockSpec((tm, D_MODEL), lambda i: (i, 0)),
        out_shape=jax.ShapeDtypeStruct((t, D_MODEL), F32),
        compiler_params=_cparams(1), name="post",
    )(x, a, r, p, wo, ln2, w1, w2, wg, wp)


def _prepare_params(ln1_w, w_in, diff_q_norm, diff_k_norm, diff_subln, ret_decay_logit, ret_gn,
                    w_out, ln2_w, w_mlp1, w_mlp2, w_ple_gate, w_ple_proj):
    depth = w_in.shape[0]
    tm = INPROJ_SUB
    dq, dk, dv, rq, rk, rv, rg = (w_in[:, :, i * 512:(i + 1) * 512] for i in range(7))
    wf = jnp.swapaxes(jnp.concatenate([dq, dk, dv, rk], axis=2), 1, 2).astype(BF16)
    wt = jnp.concatenate([rq, rv, rg], axis=2).astype(BF16)
    q_scale = (QK_DIM ** -0.5) * LOG2E
    qw_b = jnp.broadcast_to((diff_q_norm.astype(F32) * q_scale)[:, :, None], (depth, QK_DIM, tm))
    kw_b = jnp.broadcast_to(diff_k_norm.astype(F32)[:, :, None], (depth, QK_DIM, tm))
    lam_init = [0.8 - 0.6 * math.exp(-0.3 * i) for i in range(depth)]
    sub_scale = jnp.asarray([1.0 - li for li in lam_init], F32)[:, None]
    sw_b = jnp.broadcast_to((diff_subln.astype(F32) * sub_scale)[:, :, None], (depth, DV_DIM, Q_TILE))
    lgt = ret_decay_logit.astype(F32)
    lg_lane = jnp.broadcast_to(jnp.repeat(lgt, RET_HEAD_DIM, axis=2)[:, :, None, :], (depth, 2, 8, RET_WIDTH))
    lg_row = jnp.broadcast_to(jnp.repeat(lgt, RET_HEAD_DIM, axis=2)[:, :, :, None], (depth, 2, RET_WIDTH, CHUNK))
    lg_sc = jnp.broadcast_to(jnp.repeat(lgt, CHUNK, axis=2)[:, :, :, None], (depth, 2, RET_HEADS * CHUNK, CHUNK))
    gnw_b = jnp.broadcast_to(jnp.tile(ret_gn.astype(F32), (1, RET_HEADS))[:, None, :], (depth, 8, RET_WIDTH))
    return dict(
        ln1=ln1_w.astype(F32)[:, None, :], wf=wf, wt=wt, qw_b=qw_b, kw_b=kw_b, sw_b=sw_b, lam_init=lam_init,
        qw=(diff_q_norm.astype(F32) * q_scale)[:, None, :], kw=diff_k_norm.astype(F32)[:, None, :],
        lg_lane=lg_lane, lg_row=lg_row, lg_sc=lg_sc, gnw_b=gnw_b,
        wo=w_out.astype(BF16), ln2=ln2_w.astype(F32)[:, None, :], w1=w_mlp1.astype(BF16),
        w2=w_mlp2.astype(BF16), wg=w_ple_gate.astype(BF16), wp=w_ple_proj.astype(BF16))


def _run_trunk(x, p, prm, tabs, diff_lambda):
    batch, seq, _ = x.shape
    depth = p.shape[0]
    xf = x.reshape(batch * seq, D_MODEL)
    pf = p.reshape(depth, batch * seq, PLE_DIM)
    lp = diff_lambda.astype(F32)
    for i in range(depth):
        qt, k, v, rq, rkt, rv, rg = _inproj(xf, prm["ln1"], prm["wf"], prm["wt"], prm["qw_b"], prm["kw_b"],
                                            tabs, i, batch, seq)
        a = _diff_attention(lp, prm["qw"], prm["kw"], qt, k, v, prm["sw_b"], i, prm["lam_init"][i],
                            batch, seq)
        r = _retention(prm["lg_lane"], prm["lg_row"], prm["lg_sc"], prm["gnw_b"], rq, rkt, rv, rg,
                       i, batch, seq)
        xf = _post(xf, a, r, pf, prm["wo"], prm["ln2"], prm["w1"],
                   prm["w2"], prm["wg"], prm["wp"], i, batch, seq)
    return xf.reshape(batch, seq, D_MODEL)


def kernel(x_prompt, x_sample, p_prompt, p_sample, ln1_w, w_in, diff_q_norm, diff_k_norm, diff_lambda,
           diff_subln, ret_decay_logit, ret_gn, w_out, ln2_w, w_mlp1, w_mlp2, w_ple_gate, w_ple_proj):
    prm = _prepare_params(ln1_w, w_in, diff_q_norm, diff_k_norm, diff_subln, ret_decay_logit, ret_gn,
                          w_out, ln2_w, w_mlp1, w_mlp2, w_ple_gate, w_ple_proj)
    tabs = _rope_tables(max(x_prompt.shape[1], x_sample.shape[1]))
    y_prompt = _run_trunk(x_prompt, p_prompt, prm, tabs, diff_lambda)
    y_sample = _run_trunk(x_sample, p_sample, prm, tabs, diff_lambda)
    return (y_prompt, y_sample)
```
